```python
import jax, jax.numpy as jnp
from jax import lax
import numpy as np

D_MODEL = 1024
BATCH = 4
SEQ = 8192
DEPTH = 4

CHUNK = 64
GLA_HEADS = 4
GLA_DK = D_MODEL // 2
GLA_DV = D_MODEL
GLA_HEAD_DK = GLA_DK // GLA_HEADS
GLA_HEAD_DV = GLA_DV // GLA_HEADS
GATE_RANK = 16
GATE_TAU = 16.0
POOL_WIDTH = D_MODEL
POOL_WINDOWS = (2, 4, 8, 16)
POOL_GROUPS = len(POOL_WINDOWS)
POOL_GROUP_DIM = POOL_WIDTH // POOL_GROUPS
N_BRANCHES = 2
IN_SPLITS = (GLA_DK, GLA_DK, GLA_DV, GLA_DV, GATE_RANK, POOL_WIDTH, POOL_WIDTH, N_BRANCHES * D_MODEL)
IN_COLS = sum(IN_SPLITS)
IN_OFFSETS = tuple(int(v) for v in np.cumsum(IN_SPLITS)[:-1])
DEEPNORM_ALPHA = (2.0 * DEPTH) ** 0.25
DEEPNORM_BETA = (8.0 * DEPTH) ** -0.25
EPS = 1e-5

kernel_name = "hybrid_gla_pool_deepnorm_encoder"


def _layernorm(x, g, b):
    x32 = x.astype(jnp.float32)
    mu = jnp.mean(x32, axis=-1, keepdims=True)
    var = jnp.mean(jnp.square(x32 - mu), axis=-1, keepdims=True)
    return ((x32 - mu) * lax.rsqrt(var + EPS) * g + b).astype(x.dtype)


def _gla_chunked(q, k, v, log_alpha):
    B, S, H, dk = q.shape
    dv = v.shape[-1]
    nc = S // CHUNK

    def to_chunks(a):
        return a.reshape(B, nc, CHUNK, H, a.shape[-1]).transpose(1, 0, 3, 2, 4)

    qc, kc, vc, lac = to_chunks(q), to_chunks(k), to_chunks(v), to_chunks(log_alpha)

    def step(state, inp):
        qb, kb, vb, lab = inp
        G = jnp.cumsum(lab, axis=2)
        decay = jnp.exp(-jnp.abs(G[:, :, :, None, :] - G[:, :, None, :, :]))
        scores = jnp.einsum('bhtd,bhsd,bhtsd->bhts', qb, kb, decay)
        o_intra = jnp.einsum('bhts,bhsv->bhtv', scores, vb)
        o_inter = jnp.einsum('bhtd,bhdv->bhtv', qb * jnp.exp(G), state)
        G_last = G[:, :, -1:, :]
        new_state = jnp.exp(G_last)[:, :, 0, :, None] * state + jnp.einsum(
            'bhsd,bhsv->bhdv', kb * jnp.exp(G_last - G), vb)
        return new_state, o_intra + o_inter

    state0 = jnp.zeros((B, H, dk, dv), jnp.float32)
    _, o = lax.scan(step, state0, (qc, kc, vc, lac))
    return o.transpose(1, 0, 3, 2, 4).reshape(B, S, H, dv)


def _multiscale_pool(u, w_grp, scale):
    B, S, _ = u.shape
    ug = u.astype(jnp.float32).reshape(B, S, POOL_GROUPS, POOL_GROUP_DIM)
    csum = jnp.cumsum(ug, axis=1)
    pos = jnp.arange(1, S + 1)
    means = []
    for g, w in enumerate(POOL_WINDOWS):
        c = csum[:, :, g]
        shifted = jnp.pad(c, ((0, 0), (w, 0), (0, 0)))[:, :S]
        cnt = jnp.minimum(pos, w).astype(jnp.float32)
        means.append((c - shifted) / cnt[None, :, None])
    pooled = jnp.stack(means, axis=2) - ug
    mixed = jnp.einsum('bsgi,gio->bsgo', pooled, w_grp)
    return mixed.reshape(B, S, POOL_WIDTH) * scale


def _layer(x, w_in, w_alpha_up, b_alpha, gla_norm_g, w_pool_grp, pool_scale,
           b_merge, w_proj_a, w_proj_b, w_out, ln_g, ln_b):
    B, S, _ = x.shape
    h = x @ w_in
    q, k, v, gate_a, alpha_low, pool_in, gate_b, merge_logits = jnp.split(h, IN_OFFSETS, axis=-1)

    log_alpha = jax.nn.log_sigmoid((alpha_low @ w_alpha_up + b_alpha).astype(jnp.float32)) / GATE_TAU
    qh = q.reshape(B, S, GLA_HEADS, GLA_HEAD_DK) * (GLA_HEAD_DK ** -0.5)
    kh = k.reshape(B, S, GLA_HEADS, GLA_HEAD_DK)
    vh = v.reshape(B, S, GLA_HEADS, GLA_HEAD_DV)
    lah = log_alpha.reshape(B, S, GLA_HEADS, GLA_HEAD_DK)
    o = _gla_chunked(qh, kh, vh, lah)
    o = o * lax.rsqrt(jnp.mean(jnp.square(o), axis=-1, keepdims=True) + EPS) * gla_norm_g
    y_a = o.reshape(B, S, GLA_DV).astype(x.dtype) * jax.nn.silu(gate_a)

    y_b = _multiscale_pool(pool_in, w_pool_grp, pool_scale).astype(x.dtype) * jax.nn.silu(gate_b)

    gates = jax.nn.sigmoid(merge_logits + b_merge)
    g_a, g_b = jnp.split(gates, N_BRANCHES, axis=-1)
    merged = g_a * (y_a @ w_proj_a) + g_b * (y_b @ w_proj_b)
    y = merged @ w_out

    return _layernorm(DEEPNORM_ALPHA * x + y, ln_g, ln_b)


def setup_inputs(seed: int = 0) -> dict:
    key = jax.random.key(seed)
    ks = jax.random.split(key, 14)
    f32 = jnp.float32
    nrm = lambda k, shape, s: jax.random.normal(k, shape, f32) * s
    return {
        "x": jax.random.normal(ks[0], (BATCH, SEQ, D_MODEL), f32),
        "w_in": nrm(ks[1], (DEPTH, D_MODEL, IN_COLS), D_MODEL ** -0.5),
        "w_alpha_up": nrm(ks[2], (DEPTH, GATE_RANK, GLA_DK), GATE_RANK ** -0.5),
        "b_alpha": nrm(ks[3], (DEPTH, GLA_DK), 0.01),
        "gla_norm_g": 1.0 + nrm(ks[4], (DEPTH, GLA_HEADS, GLA_HEAD_DV), 0.02),
        "w_pool_grp": nrm(ks[5], (DEPTH, POOL_GROUPS, POOL_GROUP_DIM, POOL_GROUP_DIM), POOL_GROUP_DIM ** -0.5),
        "pool_scale": 1.0 + nrm(ks[6], (DEPTH, POOL_WIDTH), 0.02),
        "b_merge": nrm(ks[7], (DEPTH, N_BRANCHES * D_MODEL), 0.01),
        "w_proj_a": nrm(ks[8], (DEPTH, GLA_DV, D_MODEL), DEEPNORM_BETA * GLA_DV ** -0.5),
        "w_proj_b": nrm(ks[9], (DEPTH, POOL_WIDTH, D_MODEL), DEEPNORM_BETA * POOL_WIDTH ** -0.5),
        "w_out": nrm(ks[10], (DEPTH, D_MODEL, D_MODEL), DEEPNORM_BETA * D_MODEL ** -0.5),
        "ln_g": 1.0 + nrm(ks[11], (DEPTH, D_MODEL), 0.02),
        "ln_b": nrm(ks[12], (DEPTH, D_MODEL), 0.01),
    }


def reference(x, w_in, w_alpha_up, b_alpha, gla_norm_g, w_pool_grp, pool_scale,
              b_merge, w_proj_a, w_proj_b, w_out, ln_g, ln_b):
    for l in range(DEPTH):
        x = _layer(x, w_in[l], w_alpha_up[l], b_alpha[l], gla_norm_g[l], w_pool_grp[l],
                   pool_scale[l], b_merge[l], w_proj_a[l], w_proj_b[l], w_out[l],
                   ln_g[l], ln_b[l])
    return x
```

```python
import functools

import numpy as np
import jax
import jax.numpy as jnp
from jax import lax
from jax.experimental import pallas as pl
from jax.experimental.pallas import tpu as pltpu

D_MODEL = 1024
DEPTH = 4
CHUNK = 64
HEADS = 4
DK = D_MODEL // 2
DV = D_MODEL
HEAD_DK = DK // HEADS
HEAD_DV = DV // HEADS
GATE_RANK = 16
GATE_TAU = 16.0
POOL_WINDOWS = (2, 4, 8, 16)
POOL_GROUP_DIM = D_MODEL // len(POOL_WINDOWS)
POOL_HIST = 16
DEEPNORM_ALPHA = (2.0 * DEPTH) ** 0.25
EPS = 1e-5

LANE = 128
RANK_PAD = LANE
N_LEVELS = 6
N_SPLIT = 3
SEQ_TILE = 256
VMEM_LIMIT_BYTES = 60 * 1024 * 1024

OFF_Q = 0
OFF_K = OFF_Q + DK
OFF_V = OFF_K + DK
OFF_GA = OFF_V + DV
OFF_AL = OFF_GA + DV
OFF_PI = OFF_AL + RANK_PAD
OFF_GB = OFF_PI + D_MODEL
OFF_M = OFF_GB + D_MODEL
IN_COLS_PACKED = OFF_M + 2 * D_MODEL

F32 = jnp.float32
BF16 = jnp.bfloat16


def _decay_tables():
    c = CHUNK
    t = np.zeros((N_LEVELS + 2, c, c), np.float32)
    masks = np.zeros((N_LEVELS + 1, c, c), np.float32)
    for l in range(N_LEVELS):
        n = c >> l
        half = n // 2
        for r in range(c):
            base = (r // n) * n
            b = base + half - 1
            if r > b:
                t[l, r, b + 1:r + 1] = 1.0
            else:
                t[l, r, r + 1:b + 1] = 1.0
            for s in range(base, base + n):
                if (r > b) != (s > b):
                    masks[l, r, s] = 1.0
    for r in range(c):
        t[N_LEVELS, r, :r + 1] = 1.0
        t[N_LEVELS + 1, r, r + 1:] = 1.0
    masks[N_LEVELS] = np.eye(c, dtype=np.float32)
    tmat = t.reshape((N_LEVELS + 2) * c, c)
    tmat = np.concatenate([tmat] * N_SPLIT, axis=1)
    masks = np.concatenate([masks, masks], axis=2)
    return tmat, masks


def _pool_tables():
    inv_w = np.zeros((1, D_MODEL), np.float32)
    inv_head = np.zeros((POOL_HIST, D_MODEL), np.float32)
    for g, w in enumerate(POOL_WINDOWS):
        cols = slice(g * POOL_GROUP_DIM, (g + 1) * POOL_GROUP_DIM)
        inv_w[0, cols] = 1.0 / w
        for r in range(POOL_HIST):
            inv_head[r, cols] = 1.0 / min(r + 1, w)
    return inv_w, inv_head


def _dot(a, b):
    return jnp.dot(a, b, preferred_element_type=F32)


def _dot_nt(a, b):
    return lax.dot_general(a, b, (((1,), (1,)), ((), ())), preferred_element_type=F32)


def _dot_tn(a, b):
    return lax.dot_general(a, b, (((0,), (0,)), ((), ())), preferred_element_type=F32)


def _sigmoid(x):
    return 1.0 / (1.0 + jnp.exp(-x))


def _layer_kernel(x_ref, win_ref, wup_ref, balpha_ref, gnorm_ref, wgrp_ref, pscale_ref,
                  bmerge_ref, wpa_ref, wpb_ref, wout_ref, lng_ref, lnb_ref,
                  tmat_ref, masks_ref, invw_ref, invhead_ref,
                  out_ref,
                  q_s, k_s, v_s, la_s, o_s, state_s, pbuf_s, pooled_s):
    tile = x_ref.shape[1]
    j = pl.program_id(1)

    @pl.when(j == 0)
    def _():
        state_s[...] = jnp.zeros_like(state_s)
        pbuf_s[0:POOL_HIST, :] = jnp.zeros((POOL_HIST, D_MODEL), F32)

    x = x_ref[0]
    xb = x.astype(BF16)

    qk = _dot(xb, win_ref[:, OFF_Q:OFF_V])
    q_s[...] = qk[:, :DK] * (HEAD_DK ** -0.5)
    k_s[...] = qk[:, DK:]
    v_s[...] = _dot(xb, win_ref[:, OFF_V:OFF_GA]).astype(BF16)
    al = _dot(xb, win_ref[:, OFF_AL:OFF_PI])
    z = _dot(al.astype(BF16), wup_ref[...]) + balpha_ref[...]
    la_s[...] = (jnp.minimum(z, 0.0) - jnp.log1p(jnp.exp(-jnp.abs(z)))) * (1.0 / GATE_TAU)

    lane = lax.broadcasted_iota(jnp.int32, (CHUNK, 2 * HEAD_DK), 1)
    first_head = lane < HEAD_DK
    zeros_v = jnp.zeros((CHUNK, HEAD_DV), BF16)

    def chunk_body(c, carry):
        r0 = pl.multiple_of(c * CHUNK, CHUNK)
        rows = pl.ds(r0, CHUNK)
        la_c = la_s[rows, :]
        p0 = la_c.astype(BF16)
        rem = la_c - p0.astype(F32)
        p1 = rem.astype(BF16)
        p2 = (rem - p1.astype(F32)).astype(BF16)
        pieces = jnp.concatenate([p0, p1, p2], axis=0)
        expo = _dot(tmat_ref[...], pieces)
        dec = jnp.exp(expo)
        q_c = q_s[rows, :]
        k_c = k_s[rows, :]
        e_g = dec[N_LEVELS * CHUNK:(N_LEVELS + 1) * CHUNK, :]
        e_r = dec[(N_LEVELS + 1) * CHUNK:(N_LEVELS + 2) * CHUNK, :]
        qg = (q_c * e_g).astype(BF16)
        kr = (k_c * e_r).astype(BF16)
        a_last = e_g[CHUNK - 1:CHUNK, :]

        for hp in range(HEADS // 2):
            lanes = slice(hp * 2 * HEAD_DK, (hp + 1) * 2 * HEAD_DK)
            acc = jnp.zeros((CHUNK, 2 * CHUNK), F32)
            for l in range(N_LEVELS + 1):
                if l < N_LEVELS:
                    e_l = dec[l * CHUNK:(l + 1) * CHUNK, lanes]
                    a_l = (q_c[:, lanes] * e_l).astype(BF16)
                    b_l = (k_c[:, lanes] * e_l).astype(BF16)
                else:
                    a_l = q_c[:, lanes].astype(BF16)
                    b_l = k_c[:, lanes].astype(BF16)
                b_blk = jnp.concatenate([jnp.where(first_head, b_l, jnp.zeros_like(b_l)),
                                         jnp.where(first_head, jnp.zeros_like(b_l), b_l)], axis=0)
                acc = acc + _dot_nt(a_l, b_blk) * masks_ref[l]
            scores = acc.astype(BF16)
            for hh in range(2):
                h = hp * 2 + hh
                kl = slice(h * HEAD_DK, (h + 1) * HEAD_DK)
                vl = slice(h * HEAD_DV, (h + 1) * HEAD_DV)
                v_h = v_s[rows, vl]
                st = state_s[h]
                v_blk = (jnp.concatenate([v_h, zeros_v], axis=0) if hh == 0
                         else jnp.concatenate([zeros_v, v_h], axis=0))
                lhs = jnp.concatenate([scores, qg[:, kl]], axis=1)
                rhs = jnp.concatenate([v_blk, st.astype(BF16)], axis=0)
                o_s[rows, vl] = _dot(lhs, rhs)
                a_col = jnp.broadcast_to(a_last[:, kl], (HEAD_DK, HEAD_DK)).T
                a_col = jnp.concatenate([a_col, a_col], axis=1)
                state_s[h] = st * a_col + _dot_tn(kr[:, kl], v_h)
        return carry

    lax.fori_loop(0, tile // CHUNK, chunk_body, 0)

    gate_a = _dot(xb, win_ref[:, OFF_GA:OFF_AL])
    ya_parts = []
    for h in range(HEADS):
        vl = slice(h * HEAD_DV, (h + 1) * HEAD_DV)
        o_h = o_s[:, vl]
        ms = jnp.mean(o_h * o_h, axis=-1, keepdims=True)
        ya_parts.append(o_h * lax.rsqrt(ms + EPS) * gnorm_ref[:, vl])
    g_silu = gate_a * _sigmoid(gate_a)
    y_a = (jnp.concatenate(ya_parts, axis=1) * g_silu).astype(BF16)
    z_a = _dot(y_a, wpa_ref[...])

    u = _dot(xb, win_ref[:, OFF_PI:OFF_GB])
    pbuf_s[POOL_HIST:POOL_HIST + tile, :] = u
    for g, w in enumerate(POOL_WINDOWS):
        cols = slice(g * POOL_GROUP_DIM, (g + 1) * POOL_GROUP_DIM)
        s = pbuf_s[:, cols]
        shift = 1
        while shift < w:
            s = s + pltpu.roll(s, shift, 0)
            shift *= 2
        s = s[POOL_HIST:POOL_HIST + tile, :]
        pooled_s[:, cols] = (s * invw_ref[:, cols] - u[:, cols]).astype(BF16)

        @pl.when(j == 0)
        def _():
            pooled_s[0:POOL_HIST, cols] = (s[0:POOL_HIST, :] * invhead_ref[:, cols]
                                           - u[0:POOL_HIST, cols]).astype(BF16)
    pbuf_s[0:POOL_HIST, :] = pbuf_s[tile:tile + POOL_HIST, :]
    mixed = jnp.concatenate(
        [_dot(pooled_s[:, g * POOL_GROUP_DIM:(g + 1) * POOL_GROUP_DIM], wgrp_ref[g])
         for g in range(len(POOL_WINDOWS))], axis=1)
    gate_b = _dot(xb, win_ref[:, OFF_GB:OFF_M])
    y_b = (mixed * pscale_ref[...] * (gate_b * _sigmoid(gate_b))).astype(BF16)
    z_b = _dot(y_b, wpb_ref[...])

    gates = _sigmoid(_dot(xb, win_ref[:, OFF_M:IN_COLS_PACKED]) + bmerge_ref[...])
    merged = (gates[:, :D_MODEL] * z_a + gates[:, D_MODEL:] * z_b).astype(BF16)
    y = _dot(merged, wout_ref[...])
    r = DEEPNORM_ALPHA * x + y
    mu = jnp.mean(r, axis=-1, keepdims=True)
    d = r - mu
    var = jnp.mean(d * d, axis=-1, keepdims=True)
    out_ref[0] = d * lax.rsqrt(var + EPS) * lng_ref[...] + lnb_ref[...]


def _const_spec(shape):
    zeros = (0,) * len(shape)
    return pl.BlockSpec(shape, lambda b, j: zeros, pipeline_mode=pl.Buffered(1))


def _layer_call(x, consts, weights, tile):
    batch, seq, _ = x.shape
    operands = list(weights) + list(consts)
    in_specs = [pl.BlockSpec((1, tile, D_MODEL), lambda b, j: (b, j, 0))]
    in_specs += [_const_spec(a.shape) for a in operands]
    return pl.pallas_call(
        _layer_kernel,
        out_shape=jax.ShapeDtypeStruct(x.shape, x.dtype),
        grid=(batch, seq // tile),
        in_specs=in_specs,
        out_specs=pl.BlockSpec((1, tile, D_MODEL), lambda b, j: (b, j, 0)),
        scratch_shapes=[
            pltpu.VMEM((tile, DK), F32),
            pltpu.VMEM((tile, DK), F32),
            pltpu.VMEM((tile, DV), BF16),
            pltpu.VMEM((tile, DK), F32),
            pltpu.VMEM((tile, DV), F32),
            pltpu.VMEM((HEADS, HEAD_DK, HEAD_DV), F32),
            pltpu.VMEM((tile + POOL_HIST, D_MODEL), F32),
            pltpu.VMEM((tile, D_MODEL), BF16),
        ],
        compiler_params=pltpu.CompilerParams(
            dimension_semantics=("arbitrary", "arbitrary"),
            vmem_limit_bytes=VMEM_LIMIT_BYTES),
        name="gla_pool_layer",
    )(x, *operands)


def _pack_w_in(w):
    a0 = 2 * DK + 2 * DV
    pad = jnp.zeros((w.shape[0], RANK_PAD - GATE_RANK), w.dtype)
    return jnp.concatenate([w[:, :a0 + GATE_RANK], pad, w[:, a0 + GATE_RANK:]], axis=1).astype(BF16)


def kernel(x, w_in, w_alpha_up, b_alpha, gla_norm_g, w_pool_grp, pool_scale, b_merge,
           w_proj_a, w_proj_b, w_out, ln_g, ln_b):
    batch, seq, d = x.shape
    assert d == D_MODEL and seq % SEQ_TILE == 0 and w_in.shape[0] == DEPTH
    tmat, masks = _decay_tables()
    inv_w, inv_head = _pool_tables()
    consts = (jnp.asarray(tmat, BF16), jnp.asarray(masks, F32),
              jnp.asarray(inv_w, F32), jnp.asarray(inv_head, F32))
    for l in range(DEPTH):
        wup = jnp.concatenate(
            [w_alpha_up[l], jnp.zeros((RANK_PAD - GATE_RANK, DK), w_alpha_up.dtype)], axis=0).astype(BF16)
        weights = (
            _pack_w_in(w_in[l]), wup, b_alpha[l].reshape(1, DK),
            gla_norm_g[l].reshape(1, DV), w_pool_grp[l].astype(BF16), pool_scale[l].reshape(1, D_MODEL),
            b_merge[l].reshape(1, 2 * D_MODEL), w_proj_a[l].astype(BF16), w_proj_b[l].astype(BF16),
            w_out[l].astype(BF16), ln_g[l].reshape(1, D_MODEL), ln_b[l].reshape(1, D_MODEL))
        x = _layer_call(x, consts, weights, SEQ_TILE)
    return x
```

```python
import numpy as np
import jax
import jax.numpy as jnp
from jax import lax
from jax.experimental import pallas as pl
from jax.experimental.pallas import tpu as pltpu

D_MODEL = 1024
DEPTH = 4
CHUNK = 64
HEADS = 4
DK = D_MODEL // 2
DV = D_MODEL
HEAD_DK = DK // HEADS
HEAD_DV = DV // HEADS
GATE_RANK = 16
GATE_TAU = 16.0
POOL_WINDOWS = (2, 4, 8, 16)
POOL_GROUP_DIM = D_MODEL // len(POOL_WINDOWS)
POOL_HIST = 16
DEEPNORM_ALPHA = (2.0 * DEPTH) ** 0.25
EPS = 1e-5

LANE = 128
MXU_COLS = 256
RANK_PAD = LANE
PROJ_BLOCK = MXU_COLS
N_LEVELS = 6
N_SPLIT = 3
SEQ_TILE = 256
VMEM_LIMIT_BYTES = 60 * 1024 * 1024

OFF_Q = 0
OFF_K = OFF_Q + DK
OFF_V = OFF_K + DK
OFF_GA = OFF_V + DV
OFF_AL = OFF_GA + DV
OFF_PI = OFF_AL + RANK_PAD
OFF_GB = OFF_PI + D_MODEL
OFF_M = OFF_GB + D_MODEL
IN_COLS_PACKED = OFF_M + 2 * D_MODEL

F32 = jnp.float32
BF16 = jnp.bfloat16


def _decay_tables():
    c = CHUNK
    t = np.zeros((N_LEVELS + 2, c, c), np.float32)
    masks = np.zeros((N_LEVELS + 1, c, c), np.float32)
    for l in range(N_LEVELS):
        n = c >> l
        half = n // 2
        for r in range(c):
            base = (r // n) * n
            b = base + half - 1
            if r > b:
                t[l, r, b + 1:r + 1] = 1.0
            else:
                t[l, r, r + 1:b + 1] = 1.0
            for s in range(base, base + n):
                if (r > b) != (s > b):
                    masks[l, r, s] = 1.0
    for r in range(c):
        t[N_LEVELS, r, :r + 1] = 1.0
        t[N_LEVELS + 1, r, r + 1:] = 1.0
    masks[N_LEVELS] = np.eye(c, dtype=np.float32)
    tmat = t.reshape((N_LEVELS + 2) * c, c)
    tmat = np.concatenate([tmat] * N_SPLIT, axis=1)
    return tmat, masks


def _pool_tables():
    inv_w = np.zeros((1, D_MODEL), np.float32)
    inv_head = np.zeros((POOL_HIST, D_MODEL), np.float32)
    for g, w in enumerate(POOL_WINDOWS):
        cols = slice(g * POOL_GROUP_DIM, (g + 1) * POOL_GROUP_DIM)
        inv_w[0, cols] = 1.0 / w
        for r in range(POOL_HIST):
            inv_head[r, cols] = 1.0 / min(r + 1, w)
    return inv_w, inv_head


def _dot(a, b):
    return jnp.dot(a, b, preferred_element_type=F32)


def _dot_tn(a, b):
    return lax.dot_general(a, b, (((0,), (0,)), ((), ())), preferred_element_type=F32)


def _sigmoid(x):
    return 1.0 / (1.0 + jnp.exp(-x))


def _layer_kernel(x_ref, win_ref, wup_ref, balpha_ref, gnorm_ref, wgrp_ref, pscale_ref,
                  bmerge_ref, wpa_ref, wpb_ref, wout_ref, lng_ref, lnb_ref,
                  tmat_ref, masks_ref, invw_ref, invhead_ref,
                  out_ref,
                  q_s, k_s, v_s, o_s, state_s, pbuf_s, pooled_s):
    tile = x_ref.shape[1]
    n_chunks = tile // CHUNK
    j = pl.program_id(1)

    @pl.when(j == 0)
    def _():
        state_s[...] = jnp.zeros_like(state_s)
        pbuf_s[0:POOL_HIST, :] = jnp.zeros((POOL_HIST, D_MODEL), F32)

    x = x_ref[0]
    xb = x.astype(BF16)

    def proj(col0, ncols):
        return [lambda c0=c0: _dot(xb, win_ref[:, c0:c0 + PROJ_BLOCK])
                for c0 in range(col0, col0 + ncols, PROJ_BLOCK)]

    def cat(parts):
        return jnp.concatenate(parts, axis=1)

    qk = _dot(xb, win_ref[:, OFF_Q:OFF_V])
    q_s[...] = qk[:, :DK] * (HEAD_DK ** -0.5)
    k_s[...] = qk[:, DK:]
    v_s[...] = _dot(xb, win_ref[:, OFF_V:OFF_GA]).astype(BF16)
    al = _dot(xb, win_ref[:, OFF_AL:OFF_PI])
    u = cat([f() for f in proj(OFF_PI, D_MODEL)])
    z = _dot(al.astype(BF16), wup_ref[...]) + balpha_ref[...]
    gate_b = cat([f() for f in proj(OFF_GB, D_MODEL)])
    la = (jnp.minimum(z, 0.0) - jnp.log1p(jnp.exp(-jnp.abs(z)))) * (1.0 / GATE_TAU)
    gate_a = cat([f() for f in proj(OFF_GA, D_MODEL)])

    decs = []
    for c in range(n_chunks):
        la_c = la[c * CHUNK:(c + 1) * CHUNK, :]
        p0 = la_c.astype(BF16)
        rem = la_c - p0.astype(F32)
        p1 = rem.astype(BF16)
        p2 = (rem - p1.astype(F32)).astype(BF16)
        pieces = jnp.concatenate([p0, p1, p2], axis=0)
        decs.append(jnp.exp(_dot(tmat_ref[...], pieces)))
    merge_blocks = proj(OFF_M, 2 * D_MODEL)
    merge_parts = [f() for f in merge_blocks[:4]]

    pbuf_s[POOL_HIST:POOL_HIST + tile, :] = u
    for g, w in enumerate(POOL_WINDOWS):
        cols = slice(g * POOL_GROUP_DIM, (g + 1) * POOL_GROUP_DIM)
        s = pbuf_s[:, cols]
        shift = 1
        while shift < w:
            s = s + pltpu.roll(s, shift, 0)
            shift *= 2
        s = s[POOL_HIST:POOL_HIST + tile, :]
        inv_head = jnp.where(j == 0, invhead_ref[:, cols], invw_ref[:, cols])
        pooled_s[0:POOL_HIST, cols] = (s[0:POOL_HIST, :] * inv_head
                                       - u[0:POOL_HIST, cols]).astype(BF16)
        pooled_s[POOL_HIST:, cols] = (s[POOL_HIST:, :] * invw_ref[:, cols]
                                      - u[POOL_HIST:, cols]).astype(BF16)
    pbuf_s[0:POOL_HIST, :] = pbuf_s[tile:tile + POOL_HIST, :]

    qgs, krs, a_lasts, scores = [], [], [], []

    def intra_out(c):
        rows = slice(c * CHUNK, (c + 1) * CHUNK)
        for h in range(HEADS):
            vl = slice(h * HEAD_DV, (h + 1) * HEAD_DV)
            o_s[rows, vl] = _dot(scores[c][h], v_s[rows, vl])

    for c in range(n_chunks):
        rows = slice(c * CHUNK, (c + 1) * CHUNK)
        dec = decs[c]
        q_c = q_s[rows, :]
        k_c = k_s[rows, :]
        e_g = dec[N_LEVELS * CHUNK:(N_LEVELS + 1) * CHUNK, :]
        e_r = dec[(N_LEVELS + 1) * CHUNK:(N_LEVELS + 2) * CHUNK, :]
        qgs.append((q_c * e_g).astype(BF16))
        krs.append((k_c * e_r).astype(BF16))
        a_lasts.append(e_g[CHUNK - 1:CHUNK, :])
        sc_c = []
        for h in range(HEADS):
            lanes = slice(h * HEAD_DK, (h + 1) * HEAD_DK)
            acc = jnp.zeros((CHUNK, CHUNK), F32)
            for l in range(N_LEVELS + 1):
                if l < N_LEVELS:
                    e_l = dec[l * CHUNK:(l + 1) * CHUNK, lanes]
                    a_l = (q_c[:, lanes] * e_l).astype(BF16)
                    b_l = k_c[:, lanes] * e_l
                else:
                    a_l = q_c[:, lanes].astype(BF16)
                    b_l = k_c[:, lanes]
                acc = acc + _dot(a_l, b_l.T.astype(BF16)) * masks_ref[l]
            sc_c.append(acc.astype(BF16))
        scores.append(sc_c)
        if c >= 1:
            intra_out(c - 1)
    merge_parts += [f() for f in merge_blocks[4:6]]
    intra_out(n_chunks - 1)
    mixed = cat([_dot(pooled_s[:, g * POOL_GROUP_DIM:(g + 1) * POOL_GROUP_DIM], wgrp_ref[g])
                 for g in range(len(POOL_WINDOWS))])
    merge_parts += [f() for f in merge_blocks[6:]]
    y_b =(mixed * pscale_ref[...] * (gate_b * _sigmoid(gate_b))).astype(BF16)
    g_silu = gate_a * _sigmoid(gate_a)
    gates = _sigmoid(cat(merge_parts) + bmerge_ref[...])

    zb_blocks = [lambda c0=c0: _dot(y_b, wpb_ref[:, c0:c0 + PROJ_BLOCK])
                 for c0 in range(0, D_MODEL, PROJ_BLOCK)]
    zb_parts = []
    for c in range(n_chunks):
        rows = slice(c * CHUNK, (c + 1) * CHUNK)
        sts = [state_s[h] for h in range(HEADS)]
        for h in range(HEADS):
            kl = slice(h * HEAD_DK, (h + 1) * HEAD_DK)
            vl = slice(h * HEAD_DV, (h + 1) * HEAD_DV)
            o_s[rows, vl] += _dot(qgs[c][:, kl], sts[h].astype(BF16))
        for h in range(HEADS):
            kl = slice(h * HEAD_DK, (h + 1) * HEAD_DK)
            vl = slice(h * HEAD_DV, (h + 1) * HEAD_DV)
            a_col = jnp.broadcast_to(a_lasts[c][:, kl], (HEAD_DK, HEAD_DK)).T
            a_col = jnp.concatenate([a_col, a_col], axis=1)
            state_s[h] = sts[h] * a_col + _dot_tn(krs[c][:, kl], v_s[rows, vl])
        n_fill = (len(zb_blocks) * (c + 1)) // n_chunks - len(zb_parts)
        zb_parts += [zb_blocks[len(zb_parts) + i]() for i in range(n_fill)]
    z_b = cat(zb_parts)

    halves = [slice(i * (tile // 2), (i + 1) * (tile // 2)) for i in range(2)]

    def branch_a(rs):
        ya_parts = []
        for h in range(HEADS):
            vl = slice(h * HEAD_DV, (h + 1) * HEAD_DV)
            o_h = o_s[rs, vl]
            ms = jnp.mean(o_h * o_h, axis=-1, keepdims=True)
            ya_parts.append(o_h * lax.rsqrt(ms + EPS) * gnorm_ref[:, vl])
        y_a = (cat(ya_parts) * g_silu[rs, :]).astype(BF16)
        return _dot(y_a, wpa_ref[...])

    def merge_out(rs, z_a):
        merged = (gates[rs, :D_MODEL] * z_a + gates[rs, D_MODEL:] * z_b[rs, :]).astype(BF16)
        return _dot(merged, wout_ref[...])

    def post_norm(rs, y):
        r = DEEPNORM_ALPHA * x[rs, :] + y
        mu = jnp.mean(r, axis=-1, keepdims=True)
        d = r - mu
        var = jnp.mean(d * d, axis=-1, keepdims=True)
        out_ref[0, rs, :] = d * lax.rsqrt(var + EPS) * lng_ref[...] + lnb_ref[...]

    z_as = [branch_a(rs) for rs in halves]
    ys = [merge_out(rs, z_a) for rs, z_a in zip(halves, z_as)]
    for rs, y in zip(halves, ys):
        post_norm(rs, y)


def _const_spec(shape):
    zeros = (0,) * len(shape)
    return pl.BlockSpec(shape, lambda b, j: zeros, pipeline_mode=pl.Buffered(1))


def _layer_call(x, consts, weights, tile):
    batch, seq, _ = x.shape
    operands = list(weights) + list(consts)
    in_specs = [pl.BlockSpec((1, tile, D_MODEL), lambda b, j: (b, j, 0))]
    in_specs += [_const_spec(a.shape) for a in operands]
    return pl.pallas_call(
        _layer_kernel,
        out_shape=jax.ShapeDtypeStruct(x.shape, x.dtype),
        grid=(batch, seq // tile),
        in_specs=in_specs,
        out_specs=pl.BlockSpec((1, tile, D_MODEL), lambda b, j: (b, j, 0)),
        scratch_shapes=[
            pltpu.VMEM((tile, DK), F32),
            pltpu.VMEM((tile, DK), F32),
            pltpu.VMEM((tile, DV), BF16),
            pltpu.VMEM((tile, DV), F32),
            pltpu.VMEM((HEADS, HEAD_DK, HEAD_DV), F32),
            pltpu.VMEM((tile + POOL_HIST, D_MODEL), F32),
            pltpu.VMEM((tile, D_MODEL), BF16),
        ],
        compiler_params=pltpu.CompilerParams(
            dimension_semantics=("arbitrary", "arbitrary"),
            vmem_limit_bytes=VMEM_LIMIT_BYTES),
        name="gla_pool_layer",
    )(x, *operands)


def _pack_w_in(w):
    a0 = 2 * DK + 2 * DV
    pad = jnp.zeros((w.shape[0], RANK_PAD - GATE_RANK), w.dtype)
    return jnp.concatenate([w[:, :a0 + GATE_RANK], pad, w[:, a0 + GATE_RANK:]], axis=1).astype(BF16)


def kernel(x, w_in, w_alpha_up, b_alpha, gla_norm_g, w_pool_grp, pool_scale, b_merge,
           w_proj_a, w_proj_b, w_out, ln_g, ln_b):
    batch, seq, d = x.shape
    assert d == D_MODEL and seq % SEQ_TILE == 0 and w_in.shape[0] == DEPTH
    tmat, masks = _decay_tables()
    inv_w, inv_head = _pool_tables()
    consts = (jnp.asarray(tmat, BF16), jnp.asarray(masks, F32),
              jnp.asarray(inv_w, F32), jnp.asarray(inv_head, F32))
    for l in range(DEPTH):
        wup = jnp.concatenate(
            [w_alpha_up[l], jnp.zeros((RANK_PAD - GATE_RANK, DK), w_alpha_up.dtype)], axis=0).astype(BF16)
        weights = (
            _pack_w_in(w_in[l]), wup, b_alpha[l].reshape(1, DK),
            gla_norm_g[l].reshape(1, DV), w_pool_grp[l].astype(BF16), pool_scale[l].reshape(1, D_MODEL),
            b_merge[l].reshape(1, 2 * D_MODEL), w_proj_a[l].astype(BF16), w_proj_b[l].astype(BF16),
            w_out[l].astype(BF16), ln_g[l].reshape(1, D_MODEL), ln_b[l].reshape(1, D_MODEL))
        x = _layer_call(x, consts, weights, SEQ_TILE)
    return x
```

```python
import functools

import numpy as np
import jax
import jax.numpy as jnp
from jax import lax
from jax.experimental import pallas as pl
from jax.experimental.pallas import tpu as pltpu

D_MODEL = 1024
DEPTH = 4
CHUNK = 64
HEADS = 4
DK = D_MODEL // 2
DV = D_MODEL
HEAD_DK = DK // HEADS
HEAD_DV = DV // HEADS
GATE_RANK = 16
GATE_TAU = 16.0
POOL_WINDOWS = (2, 4, 8, 16)
POOL_GROUP_DIM = D_MODEL // len(POOL_WINDOWS)
POOL_HIST = 16
DEEPNORM_ALPHA = (2.0 * DEPTH) ** 0.25
EPS = 1e-5

LANE = 128
MXU_COLS = 256
RANK_PAD = LANE
PROJ_BLOCK = MXU_COLS
N_LEVELS = 6
SUBLANES = 8
VEC_LEVELS = 3
N_SPLIT = 3
SEQ_TILE = 256
VMEM_LIMIT_BYTES = 60 * 1024 * 1024

OFF_Q = 0
OFF_K = OFF_Q + DK
OFF_V = OFF_K + DK
OFF_GA = OFF_V + DV
OFF_AL = OFF_GA + DV
A_COLS = OFF_AL + GATE_RANK
OFF_PI = 0
OFF_GB = OFF_PI + D_MODEL
OFF_M = OFF_GB + D_MODEL
B_COLS = OFF_M + 2 * D_MODEL


def _padded_cols(n):
    tiles = -(-n // LANE)
    return (tiles + 1 - tiles % 2) * LANE

F32 = jnp.float32
BF16 = jnp.bfloat16


def _decay_tables():
    c = CHUNK
    t = np.zeros((N_LEVELS + 1, c, c), np.float32)
    masks = np.zeros((N_LEVELS + 1, c, c), np.float32)
    for l in range(N_LEVELS):
        n = c >> l
        half = n // 2
        for r in range(c):
            base = (r // n) * n
            b = base + half - 1
            if r > b:
                t[l, r, b + 1:r + 1] = 1.0
            else:
                t[l, r, r + 1:b + 1] = 1.0
            for s in range(base, base + n):
                if (r > b) != (s > b):
                    masks[l, r, s] = 1.0
    for r in range(c):
        t[N_LEVELS, r, :r + 1] = 1.0
    masks[N_LEVELS] = np.eye(c, dtype=np.float32)
    tmat = t[VEC_LEVELS:].reshape((N_LEVELS + 1 - VEC_LEVELS) * c, c)
    tmat = np.concatenate([tmat] * N_SPLIT, axis=1)
    return tmat, masks


def _pool_tables():
    inv_w = np.zeros((1, D_MODEL), np.float32)
    inv_head = np.zeros((POOL_HIST, D_MODEL), np.float32)
    for g, w in enumerate(POOL_WINDOWS):
        cols = slice(g * POOL_GROUP_DIM, (g + 1) * POOL_GROUP_DIM)
        inv_w[0, cols] = 1.0 / w
        for r in range(POOL_HIST):
            inv_head[r, cols] = 1.0 / min(r + 1, w)
    return inv_w, inv_head


def _dot(a, b):
    return jnp.dot(a, b, preferred_element_type=F32)


def _dot_tn(a, b):
    return lax.dot_general(a, b, (((0,), (0,)), ((), ())), preferred_element_type=F32)


def _block_exponents(g, level):
    n = CHUNK >> level
    half = n // 2
    assert half % SUBLANES == 0
    parts = []
    for base in range(0, CHUNK, n):
        g_b = g[base + half - 1:base + half, :]
        parts.append(g_b - g[base:base + half, :])
        parts.append(g[base + half:base + n, :] - g_b)
    return jnp.concatenate(parts, axis=0)


def _sigmoid(x):
    return 1.0 / (1.0 + jnp.exp(-x))


def _layer_kernel(x_ref, wa_ref, wb_ref, wup_ref, balpha_ref, gnorm_ref, wgrp_ref,
                  pscale_ref, bmerge_ref, wpa_ref, wpb_ref, wout_ref, lng_ref, lnb_ref,
                  tmat_ref, masks_ref, invw_ref, invhead_ref,
                  out_ref,
                  q_s, k_s, v_s, o_s, state_s, pbuf_s, pooled_s,
                  gsilu_s, ga_s, gbzb_s, xprev_s, *, tiles_per_seq):
    tile = x_ref.shape[1]
    n_chunks = tile // CHUNK
    step = pl.program_id(0)
    j = lax.rem(step, tiles_per_seq)

    @pl.when(step == 0)
    def _():
        for ref in (o_s, gsilu_s, ga_s, gbzb_s, xprev_s):
            ref[...] = jnp.zeros_like(ref)

    @pl.when(j == 0)
    def _():
        state_s[...] = jnp.zeros_like(state_s)
        pbuf_s[0:POOL_HIST, :] = jnp.zeros((POOL_HIST, D_MODEL), F32)

    x = x_ref[0]
    xb = x.astype(BF16)
    halves = [slice(i * (tile // 2), (i + 1) * (tile // 2)) for i in range(2)]

    def branch_a(rs):
        ya_parts = []
        for h in range(HEADS):
            vl = slice(h * HEAD_DV, (h + 1) * HEAD_DV)
            o_h = o_s[rs, vl]
            ms = jnp.mean(o_h * o_h, axis=-1, keepdims=True)
            ya_parts.append(o_h * lax.rsqrt(ms + EPS) * gnorm_ref[:, vl])
        y_a = (jnp.concatenate(ya_parts, axis=1) * gsilu_s[rs, :]).astype(BF16)
        return _dot(y_a, wpa_ref[:, :D_MODEL])

    def merge_out(rs, z_a):
        merged = (ga_s[rs, :] * z_a + gbzb_s[rs, :]).astype(BF16)
        return _dot(merged, wout_ref[:, :D_MODEL])

    def post_norm(rs, y):
        r = DEEPNORM_ALPHA * xprev_s[rs, :] + y
        mu = jnp.mean(r, axis=-1, keepdims=True)
        d = r - mu
        var = jnp.mean(d * d, axis=-1, keepdims=True)
        out_ref[0, rs, :] = d * lax.rsqrt(var + EPS) * lng_ref[...] + lnb_ref[...]

    def proj(w_ref, col0, ncols):
        return [lambda c0=c0: _dot(xb, w_ref[:, c0:c0 + PROJ_BLOCK])
                for c0 in range(col0, col0 + ncols, PROJ_BLOCK)]

    def cat(parts):
        return jnp.concatenate(parts, axis=1)

    qk = _dot(xb, wa_ref[:, OFF_Q:OFF_V])
    q_s[...] = qk[:, :DK] * (HEAD_DK ** -0.5)
    k_s[...] = qk[:, DK:]
    v_s[...] = _dot(xb, wa_ref[:, OFF_V:OFF_GA]).astype(BF16)
    al = _dot(xb, wa_ref[:, OFF_AL:OFF_AL + RANK_PAD])
    z_as = [branch_a(rs) for rs in halves]
    u = cat([f() for f in proj(wb_ref, OFF_PI, D_MODEL)])
    z = _dot(al.astype(BF16), wup_ref[...]) + balpha_ref[...]
    gate_b = cat([f() for f in proj(wb_ref, OFF_GB, D_MODEL)])
    la = (jnp.minimum(z, 0.0) - jnp.log1p(jnp.exp(-jnp.abs(z)))) * (1.0 / GATE_TAU)
    silu_b = gate_b * _sigmoid(gate_b)
    ys = [merge_out(rs, z_a) for rs, z_a in zip(halves, z_as)]
    gate_a = cat([f() for f in proj(wa_ref, OFF_GA, D_MODEL)])
    gsilu_s[...] = gate_a * _sigmoid(gate_a)

    n_fine = N_LEVELS - VEC_LEVELS
    decs, e_gs, e_rs = [], [], []
    for c in range(n_chunks):
        la_c = la[c * CHUNK:(c + 1) * CHUNK, :]
        p0 = la_c.astype(BF16)
        rem = la_c - p0.astype(F32)
        p1 = rem.astype(BF16)
        p2 = (rem - p1.astype(F32)).astype(BF16)
        pieces = jnp.concatenate([p0, p1, p2], axis=0)
        expo = _dot(tmat_ref[...], pieces)
        g = expo[n_fine * CHUNK:, :]
        fine = jnp.exp(expo)
        decs.append([jnp.exp(_block_exponents(g, l)) for l in range(VEC_LEVELS)]
                    + [fine[i * CHUNK:(i + 1) * CHUNK, :] for i in range(n_fine)])
        e_gs.append(fine[n_fine * CHUNK:, :])
        e_rs.append(jnp.exp(g[CHUNK - 1:CHUNK, :] - g))
    merge_blocks = proj(wb_ref, OFF_M, 2 * D_MODEL)
    merge_a = cat([f() for f in merge_blocks[:4]])
    for rs, y in zip(halves, ys):
        post_norm(rs, y)
    xprev_s[...] = x
    ga_s[...] = _sigmoid(merge_a + bmerge_ref[:, :D_MODEL])

    pbuf_s[POOL_HIST:POOL_HIST + tile, :] = u
    for g, w in enumerate(POOL_WINDOWS):
        cols = slice(g * POOL_GROUP_DIM, (g + 1) * POOL_GROUP_DIM)
        s = pbuf_s[:, cols]
        shift = 1
        while shift < w:
            s = s + pltpu.roll(s, shift, 0)
            shift *= 2
        s = s[POOL_HIST:POOL_HIST + tile, :]
        inv_head = jnp.where(j == 0, invhead_ref[:, cols], invw_ref[:, cols])
        pooled_s[0:POOL_HIST, cols] = (s[0:POOL_HIST, :] * inv_head
                                       - u[0:POOL_HIST, cols]).astype(BF16)
        pooled_s[POOL_HIST:, cols] = (s[POOL_HIST:, :] * invw_ref[:, cols]
                                      - u[POOL_HIST:, cols]).astype(BF16)
    pbuf_s[0:POOL_HIST, :] = pbuf_s[tile:tile + POOL_HIST, :]

    qgs, krs, a_lasts, scores = [], [], [], []

    def intra_out(c):
        rows = slice(c * CHUNK, (c + 1) * CHUNK)
        for h in range(HEADS):
            vl = slice(h * HEAD_DV, (h + 1) * HEAD_DV)
            o_s[rows, vl] = _dot(scores[c][h], v_s[rows, vl])

    for c in range(n_chunks):
        rows = slice(c * CHUNK, (c + 1) * CHUNK)
        dec = decs[c]
        q_c = q_s[rows, :]
        k_c = k_s[rows, :]
        qgs.append((q_c * e_gs[c]).astype(BF16))
        krs.append((k_c * e_rs[c]).astype(BF16))
        a_lasts.append(e_gs[c][CHUNK - 1:CHUNK, :])
        sc_c = []
        q_cb = q_c.astype(BF16)
        k_cb = k_c.astype(BF16)
        dec_b = [d.astype(BF16) for d in dec]
        for h in range(HEADS):
            lanes = slice(h * HEAD_DK, (h + 1) * HEAD_DK)
            acc = jnp.zeros((CHUNK, CHUNK), F32)
            for l in range(N_LEVELS + 1):
                if l < N_LEVELS:
                    a_l = q_cb[:, lanes] * dec_b[l][:, lanes]
                    b_l = k_cb[:, lanes] * dec_b[l][:, lanes]
                else:
                    a_l = q_cb[:, lanes]
                    b_l = k_cb[:, lanes]
                acc = acc + _dot(a_l, b_l.T) * masks_ref[l]
            sc_c.append(acc.astype(BF16))
        scores.append(sc_c)
        if c >= 1:
            intra_out(c - 1)
    merge_b = [f() for f in merge_blocks[4:6]]
    intra_out(n_chunks - 1)
    mixed = cat([_dot(pooled_s[:, g * POOL_GROUP_DIM:(g + 1) * POOL_GROUP_DIM], wgrp_ref[g])
                 for g in range(len(POOL_WINDOWS))])
    merge_b += [f() for f in merge_blocks[6:]]
    y_b = (mixed * pscale_ref[...] * silu_b).astype(BF16)
    gate_mb = _sigmoid(cat(merge_b) + bmerge_ref[:, D_MODEL:])

    zb_blocks = [lambda c0=c0: _dot(y_b, wpb_ref[:, c0:c0 + PROJ_BLOCK])
                 for c0 in range(0, D_MODEL, PROJ_BLOCK)]
    zb_parts = []
    for c in range(n_chunks):
        rows = slice(c * CHUNK, (c + 1) * CHUNK)
        sts = [state_s[h] for h in range(HEADS)]
        for h in range(HEADS):
            kl = slice(h * HEAD_DK, (h + 1) * HEAD_DK)
            vl = slice(h * HEAD_DV, (h + 1) * HEAD_DV)
            o_s[rows, vl] += _dot(qgs[c][:, kl], sts[h].astype(BF16))
        for h in range(HEADS):
            kl = slice(h * HEAD_DK, (h + 1) * HEAD_DK)
            vl = slice(h * HEAD_DV, (h + 1) * HEAD_DV)
            a_col = jnp.broadcast_to(a_lasts[c][:, kl], (HEAD_DK, HEAD_DK)).T
            a_col = jnp.concatenate([a_col, a_col], axis=1)
            state_s[h] = sts[h] * a_col + _dot_tn(krs[c][:, kl], v_s[rows, vl])
        n_fill = (len(zb_blocks) * (c + 1)) // n_chunks - len(zb_parts)
        zb_parts += [zb_blocks[len(zb_parts) + i]() for i in range(n_fill)]
    z_b = cat(zb_parts)

    gbzb_s[...] = gate_mb * z_b


def _const_spec(shape):
    zeros = (0,) * len(shape)
    return pl.BlockSpec(shape, lambda s: zeros, pipeline_mode=pl.Buffered(1))


def _layer_spec(shape, layer):
    zeros = (0,) * (len(shape) - 1)
    return pl.BlockSpec((None,) + tuple(shape[1:]), lambda s: (layer,) + zeros,
                        pipeline_mode=pl.Buffered(1))


def _layer_call(x, layer, params, consts, tile):
    batch, seq, _ = x.shape
    tiles_per_seq = seq // tile
    n_tiles = batch * tiles_per_seq

    def tile_index(t):
        return (t // tiles_per_seq, t % tiles_per_seq, 0)

    in_specs = [pl.BlockSpec((1, tile, D_MODEL), lambda s: tile_index(jnp.minimum(s, n_tiles - 1)))]
    in_specs += [_layer_spec(a.shape, layer) for a in params]
    in_specs += [_const_spec(a.shape) for a in consts]
    return pl.pallas_call(
        functools.partial(_layer_kernel, tiles_per_seq=tiles_per_seq),
        out_shape=jax.ShapeDtypeStruct(x.shape, x.dtype),
        grid=(n_tiles + 1,),
        in_specs=in_specs,
        out_specs=pl.BlockSpec((1, tile, D_MODEL), lambda s: tile_index(jnp.maximum(s - 1, 0))),
        scratch_shapes=[
            pltpu.VMEM((tile, DK), F32),
            pltpu.VMEM((tile, DK), F32),
            pltpu.VMEM((tile, DV), BF16),
            pltpu.VMEM((tile, DV), F32),
            pltpu.VMEM((HEADS, HEAD_DK, HEAD_DV), F32),
            pltpu.VMEM((tile + POOL_HIST, D_MODEL), F32),
            pltpu.VMEM((tile, D_MODEL), BF16),
            pltpu.VMEM((tile, D_MODEL), F32),
            pltpu.VMEM((tile, D_MODEL), F32),
            pltpu.VMEM((tile, D_MODEL), F32),
            pltpu.VMEM((tile, D_MODEL), F32),
        ],
        compiler_params=pltpu.CompilerParams(
            dimension_semantics=("arbitrary",),
            vmem_limit_bytes=VMEM_LIMIT_BYTES),
        name="gla_pool_layer",
    )(x, *params, *consts)


def kernel(x, w_in, w_alpha_up, b_alpha, gla_norm_g, w_pool_grp, pool_scale, b_merge,
           w_proj_a, w_proj_b, w_out, ln_g, ln_b):
    batch, seq, d = x.shape
    depth = w_in.shape[0]
    assert d == D_MODEL and seq % SEQ_TILE == 0 and depth == DEPTH
    tmat, masks = _decay_tables()
    inv_w, inv_head = _pool_tables()
    consts = (jnp.asarray(tmat, BF16), jnp.asarray(masks, F32),
              jnp.asarray(inv_w, F32), jnp.asarray(inv_head, F32))
    def packed(w, n_cols=None):
        n_cols = w.shape[-1] if n_cols is None else n_cols
        pad = _padded_cols(n_cols) - w.shape[-1]
        return jnp.pad(w, ((0, 0), (0, 0), (0, pad))).astype(BF16)

    w_up = jnp.pad(w_alpha_up, ((0, 0), (0, RANK_PAD - GATE_RANK), (0, 0)))
    params = (
        packed(w_in[:, :, :A_COLS], OFF_AL + RANK_PAD), packed(w_in[:, :, A_COLS:]), w_up.astype(BF16),
        b_alpha.reshape(depth, 1, DK), gla_norm_g.reshape(depth, 1, DV),
        w_pool_grp.astype(BF16), pool_scale.reshape(depth, 1, D_MODEL),
        b_merge.reshape(depth, 1, 2 * D_MODEL),
        packed(w_proj_a), packed(w_proj_b), packed(w_out),
        ln_g.reshape(depth, 1, D_MODEL), ln_b.reshape(depth, 1, D_MODEL))
    for layer in range(depth):
        x = _layer_call(x, layer, params, consts, SEQ_TILE)
    return x
```

```python
import functools

import numpy as np
import jax
import jax.numpy as jnp
from jax import lax
from jax.experimental import pallas as pl
from jax.experimental.pallas import tpu as pltpu

D_MODEL = 1024
DEPTH = 4
CHUNK = 64
HEADS = 4
DK = D_MODEL // 2
DV = D_MODEL
HEAD_DK = DK // HEADS
HEAD_DV = DV // HEADS
GATE_RANK = 16
GATE_TAU = 16.0
POOL_WINDOWS = (2, 4, 8, 16)
POOL_GROUP_DIM = D_MODEL // len(POOL_WINDOWS)
POOL_HIST = 16
DEEPNORM_ALPHA = (2.0 * DEPTH) ** 0.25
EPS = 1e-5

LANE = 128
MXU_COLS = 256
RANK_PAD = LANE
PROJ_BLOCK = MXU_COLS
N_LEVELS = 6
SUBLANES = 8
VEC_LEVELS = 3
N_SPLIT = 3
SEQ_TILE = 256
EPILOGUE_PARTS = 2
VMEM_LIMIT_BYTES = 60 * 1024 * 1024

OFF_Q = 0
OFF_K = OFF_Q + DK
OFF_V = OFF_K + DK
OFF_GA = OFF_V + DV
OFF_AL = OFF_GA + DV
A_COLS = OFF_AL + GATE_RANK
OFF_PI = 0
OFF_GB = OFF_PI + D_MODEL
OFF_M = OFF_GB + D_MODEL
B_COLS = OFF_M + 2 * D_MODEL


def _padded_cols(n):
    tiles = -(-n // LANE)
    return (tiles + 1 - tiles % 2) * LANE

F32 = jnp.float32
BF16 = jnp.bfloat16


def _decay_tables():
    c = CHUNK
    t = np.zeros((N_LEVELS + 1, c, c), np.float32)
    masks = np.zeros((N_LEVELS + 1, c, c), np.float32)
    for l in range(N_LEVELS):
        n = c >> l
        half = n // 2
        for r in range(c):
            base = (r // n) * n
            b = base + half - 1
            if r > b:
                t[l, r, b + 1:r + 1] = 1.0
            else:
                t[l, r, r + 1:b + 1] = 1.0
            for s in range(base, base + n):
                if (r > b) != (s > b):
                    masks[l, r, s] = 1.0
    for r in range(c):
        t[N_LEVELS, r, :r + 1] = 1.0
    masks[N_LEVELS] = np.eye(c, dtype=np.float32)
    tmat = np.concatenate([t[VEC_LEVELS:N_LEVELS - 1], t[N_LEVELS:]]).reshape(-1, c)
    tmat = np.concatenate([tmat] * N_SPLIT, axis=1)
    return tmat, masks


def _pool_tables():
    inv_w = np.zeros((1, D_MODEL), np.float32)
    inv_head = np.zeros((POOL_HIST, D_MODEL), np.float32)
    for g, w in enumerate(POOL_WINDOWS):
        cols = slice(g * POOL_GROUP_DIM, (g + 1) * POOL_GROUP_DIM)
        inv_w[0, cols] = 1.0 / w
        for r in range(POOL_HIST):
            inv_head[r, cols] = 1.0 / min(r + 1, w)
    return inv_w, inv_head


def _dot(a, b):
    return jnp.dot(a, b, preferred_element_type=F32)


def _dot_tn(a, b):
    return lax.dot_general(a, b, (((0,), (0,)), ((), ())), preferred_element_type=F32)


def _block_exponents(g, level):
    n = CHUNK >> level
    half = n // 2
    assert half % SUBLANES == 0
    parts = []
    for base in range(0, CHUNK, n):
        g_b = g[base + half - 1:base + half, :]
        parts.append(g_b - g[base:base + half, :])
        parts.append(g[base + half:base + n, :] - g_b)
    return jnp.concatenate(parts, axis=0)


def _sigmoid(x):
    return 1.0 / (1.0 + jnp.exp(-x))


def _layer_kernel(x_ref, wa_ref, wb_ref, wup_ref, balpha_ref, gnorm_ref, wgrp_ref,
                  pscale_ref, bmerge_ref, wpa_ref, wpb_ref, wout_ref, lng_ref, lnb_ref,
                  tmat_ref, masks_ref, invw_ref, invhead_ref,
                  out_ref,
                  q_s, k_s, v_s, o_s, state_s, pbuf_s, pooled_s,
                  gsilu_s, ga_s, gbzb_s, xprev_s, *, tiles_per_seq):
    tile = x_ref.shape[1]
    n_chunks = tile // CHUNK
    step = pl.program_id(0)
    j = lax.rem(step, tiles_per_seq)

    @pl.when(step == 0)
    def _():
        for ref in (o_s, gsilu_s, ga_s, gbzb_s, xprev_s):
            ref[...] = jnp.zeros_like(ref)

    @pl.when(j == 0)
    def _():
        state_s[...] = jnp.zeros_like(state_s)
        pbuf_s[0:POOL_HIST, :] = jnp.zeros((POOL_HIST, D_MODEL), F32)

    x = x_ref[0]
    xb = x.astype(BF16)
    halves = [slice(i * (tile // EPILOGUE_PARTS), (i + 1) * (tile // EPILOGUE_PARTS))
              for i in range(EPILOGUE_PARTS)]

    def branch_a(rs):
        ya_parts = []
        for h in range(HEADS):
            vl = slice(h * HEAD_DV, (h + 1) * HEAD_DV)
            o_h = o_s[rs, vl]
            ms = jnp.mean(o_h * o_h, axis=-1, keepdims=True)
            ya_parts.append(o_h * lax.rsqrt(ms + EPS) * gnorm_ref[:, vl])
        y_a = (jnp.concatenate(ya_parts, axis=1) * gsilu_s[rs, :]).astype(BF16)
        return _dot(y_a, wpa_ref[:, :D_MODEL])

    def merge_out(rs, z_a):
        merged = (ga_s[rs, :] * z_a + gbzb_s[rs, :]).astype(BF16)
        return _dot(merged, wout_ref[:, :D_MODEL])

    def post_norm(rs, y):
        r = DEEPNORM_ALPHA * xprev_s[rs, :] + y
        mu = jnp.mean(r, axis=-1, keepdims=True)
        d = r - mu
        var = jnp.mean(d * d, axis=-1, keepdims=True)
        out_ref[0, rs, :] = d * lax.rsqrt(var + EPS) * lng_ref[...] + lnb_ref[...]

    def proj(w_ref, col0, ncols):
        return [lambda c0=c0: _dot(xb, w_ref[:, c0:c0 + PROJ_BLOCK])
                for c0 in range(col0, col0 + ncols, PROJ_BLOCK)]

    def cat(parts):
        return jnp.concatenate(parts, axis=1)

    qk = _dot(xb, wa_ref[:, OFF_Q:OFF_V])
    q_s[...] = qk[:, :DK] * (HEAD_DK ** -0.5)
    k_s[...] = qk[:, DK:]
    v_s[...] = _dot(xb, wa_ref[:, OFF_V:OFF_GA]).astype(BF16)
    al = _dot(xb, wa_ref[:, OFF_AL:OFF_AL + RANK_PAD])
    z_as = [branch_a(rs) for rs in halves]
    u = cat([f() for f in proj(wb_ref, OFF_PI, D_MODEL)])
    z = _dot(al.astype(BF16), wup_ref[...]) + balpha_ref[...]
    gate_b = cat([f() for f in proj(wb_ref, OFF_GB, D_MODEL)])
    la = (jnp.minimum(z, 0.0) - jnp.log1p(jnp.exp(-jnp.abs(z)))) * (1.0 / GATE_TAU)
    silu_b = gate_b * _sigmoid(gate_b)
    ys = [merge_out(rs, z_a) for rs, z_a in zip(halves, z_as)]
    gate_a = cat([f() for f in proj(wa_ref, OFF_GA, D_MODEL)])
    gsilu_s[...] = gate_a * _sigmoid(gate_a)

    n_fine = N_LEVELS - 1 - VEC_LEVELS
    odd_row = (lax.broadcasted_iota(jnp.int32, (CHUNK, DK), 0) & 1) == 1
    decs, e_gs, e_rs = [], [], []
    for c in range(n_chunks):
        la_c = la[c * CHUNK:(c + 1) * CHUNK, :]
        p0 = la_c.astype(BF16)
        rem = la_c - p0.astype(F32)
        p1 = rem.astype(BF16)
        p2 = (rem - p1.astype(F32)).astype(BF16)
        pieces = jnp.concatenate([p0, p1, p2], axis=0)
        expo = _dot(tmat_ref[...], pieces)
        g = expo[n_fine * CHUNK:, :]
        fine = jnp.exp(expo)
        pairs = jnp.where(odd_row, jnp.exp(la_c), 1.0)
        decs.append([jnp.exp(_block_exponents(g, l)) for l in range(VEC_LEVELS)]
                    + [fine[i * CHUNK:(i + 1) * CHUNK, :] for i in range(n_fine)] + [pairs])
        e_gs.append(fine[n_fine * CHUNK:, :])
        e_rs.append(jnp.exp(g[CHUNK - 1:CHUNK, :] - g))
    merge_blocks = proj(wb_ref, OFF_M, 2 * D_MODEL)
    merge_a = cat([f() for f in merge_blocks[:4]])
    for rs, y in zip(halves, ys):
        post_norm(rs, y)
    xprev_s[...] = x
    ga_s[...] = _sigmoid(merge_a + bmerge_ref[:, :D_MODEL])

    pbuf_s[POOL_HIST:POOL_HIST + tile, :] = u
    for g, w in enumerate(POOL_WINDOWS):
        cols = slice(g * POOL_GROUP_DIM, (g + 1) * POOL_GROUP_DIM)
        s = pbuf_s[:, cols]
        shift = 1
        while shift < w:
            s = s + pltpu.roll(s, shift, 0)
            shift *= 2
        s = s[POOL_HIST:POOL_HIST + tile, :]
        inv_head = jnp.where(j == 0, invhead_ref[:, cols], invw_ref[:, cols])
        pooled_s[0:POOL_HIST, cols] = (s[0:POOL_HIST, :] * inv_head
                                       - u[0:POOL_HIST, cols]).astype(BF16)
        pooled_s[POOL_HIST:, cols] = (s[POOL_HIST:, :] * invw_ref[:, cols]
                                      - u[POOL_HIST:, cols]).astype(BF16)
    pbuf_s[0:POOL_HIST, :] = pbuf_s[tile:tile + POOL_HIST, :]

    qgs, krs, a_lasts, scores = [], [], [], []
    for c in range(n_chunks):
        rows = slice(c * CHUNK, (c + 1) * CHUNK)
        dec = decs[c]
        q_c = q_s[rows, :]
        k_c = k_s[rows, :]
        qgs.append((q_c * e_gs[c]).astype(BF16))
        krs.append((k_c * e_rs[c]).astype(BF16))
        a_lasts.append(e_gs[c][CHUNK - 1:CHUNK, :])
        sc_c = []
        q_cb = q_c.astype(BF16)
        k_cb = k_c.astype(BF16)
        dec_b = [d.astype(BF16) for d in dec]
        for h in range(HEADS):
            lanes = slice(h * HEAD_DK, (h + 1) * HEAD_DK)
            acc = jnp.zeros((CHUNK, CHUNK), F32)
            for l in range(N_LEVELS + 1):
                if l < N_LEVELS:
                    a_l = q_cb[:, lanes] * dec_b[l][:, lanes]
                    b_l = k_cb[:, lanes] * dec_b[l][:, lanes]
                else:
                    a_l = q_cb[:, lanes]
                    b_l = k_cb[:, lanes]
                acc = acc + _dot(a_l, b_l.T) * masks_ref[l]
            sc_c.append(acc.astype(BF16))
        scores.append(sc_c)
    merge_b = [f() for f in merge_blocks[4:6]]
    mixed = cat([_dot(pooled_s[:, g * POOL_GROUP_DIM:(g + 1) * POOL_GROUP_DIM], wgrp_ref[g])
                 for g in range(len(POOL_WINDOWS))])
    merge_b += [f() for f in merge_blocks[6:]]
    y_b = (mixed * pscale_ref[...] * silu_b).astype(BF16)
    gate_mb = _sigmoid(cat(merge_b) + bmerge_ref[:, D_MODEL:])

    zb_blocks = [lambda c0=c0: _dot(y_b, wpb_ref[:, c0:c0 + PROJ_BLOCK])
                 for c0 in range(0, D_MODEL, PROJ_BLOCK)]
    zb_parts = []
    for c in range(n_chunks):
        rows = slice(c * CHUNK, (c + 1) * CHUNK)
        sts = [state_s[h] for h in range(HEADS)]
        for h in range(HEADS):
            kl = slice(h * HEAD_DK, (h + 1) * HEAD_DK)
            vl = slice(h * HEAD_DV, (h + 1) * HEAD_DV)
            lhs = jnp.concatenate([qgs[c][:, kl], scores[c][h]], axis=1)
            rhs = jnp.concatenate([sts[h].astype(BF16), v_s[rows, vl]], axis=0)
            o_s[rows, vl] = _dot(lhs, rhs)
        for h in range(HEADS):
            kl = slice(h * HEAD_DK, (h + 1) * HEAD_DK)
            vl = slice(h * HEAD_DV, (h + 1) * HEAD_DV)
            a_col = jnp.broadcast_to(a_lasts[c][:, kl], (HEAD_DK, HEAD_DK)).T
            a_col = jnp.concatenate([a_col, a_col], axis=1)
            state_s[h] = sts[h] * a_col + _dot_tn(krs[c][:, kl], v_s[rows, vl])
        n_fill = (len(zb_blocks) * (c + 1)) // n_chunks - len(zb_parts)
        zb_parts += [zb_blocks[len(zb_parts) + i]() for i in range(n_fill)]
    z_b = cat(zb_parts)

    gbzb_s[...] = gate_mb * z_b


def _const_spec(shape):
    zeros = (0,) * len(shape)
    return pl.BlockSpec(shape, lambda s: zeros, pipeline_mode=pl.Buffered(1))


def _layer_spec(shape, layer):
    zeros = (0,) * (len(shape) - 1)
    return pl.BlockSpec((None,) + tuple(shape[1:]), lambda s: (layer,) + zeros,
                        pipeline_mode=pl.Buffered(1))


def _layer_call(x, layer, params, consts, tile):
    batch, seq, _ = x.shape
    tiles_per_seq = seq // tile
    n_tiles = batch * tiles_per_seq

    def tile_index(t):
        return (t // tiles_per_seq, t % tiles_per_seq, 0)

    in_specs = [pl.BlockSpec((1, tile, D_MODEL), lambda s: tile_index(jnp.minimum(s, n_tiles - 1)))]
    in_specs += [_layer_spec(a.shape, layer) for a in params]
    in_specs += [_const_spec(a.shape) for a in consts]
    return pl.pallas_call(
        functools.partial(_layer_kernel, tiles_per_seq=tiles_per_seq),
        out_shape=jax.ShapeDtypeStruct(x.shape, x.dtype),
        grid=(n_tiles + 1,),
        in_specs=in_specs,
        out_specs=pl.BlockSpec((1, tile, D_MODEL), lambda s: tile_index(jnp.maximum(s - 1, 0))),
        scratch_shapes=[
            pltpu.VMEM((tile, DK), F32),
            pltpu.VMEM((tile, DK), F32),
            pltpu.VMEM((tile, DV), BF16),
            pltpu.VMEM((tile, DV), F32),
            pltpu.VMEM((HEADS, HEAD_DK, HEAD_DV), F32),
            pltpu.VMEM((tile + POOL_HIST, D_MODEL), F32),
            pltpu.VMEM((tile, D_MODEL), BF16),
            pltpu.VMEM((tile, D_MODEL), F32),
            pltpu.VMEM((tile, D_MODEL), F32),
            pltpu.VMEM((tile, D_MODEL), F32),
            pltpu.VMEM((tile, D_MODEL), F32),
        ],
        compiler_params=pltpu.CompilerParams(
            dimension_semantics=("arbitrary",),
            vmem_limit_bytes=VMEM_LIMIT_BYTES),
        name="gla_pool_layer",
    )(x, *params, *consts)


def kernel(x, w_in, w_alpha_up, b_alpha, gla_norm_g, w_pool_grp, pool_scale, b_merge,
           w_proj_a, w_proj_b, w_out, ln_g, ln_b):
    batch, seq, d = x.shape
    depth = w_in.shape[0]
    assert d == D_MODEL and seq % SEQ_TILE == 0 and depth == DEPTH
    tmat, masks = _decay_tables()
    inv_w, inv_head = _pool_tables()
    consts = (jnp.asarray(tmat, BF16), jnp.asarray(masks, F32),
              jnp.asarray(inv_w, F32), jnp.asarray(inv_head, F32))
    def packed(w, n_cols=None):
        n_cols = w.shape[-1] if n_cols is None else n_cols
        pad = _padded_cols(n_cols) - w.shape[-1]
        return jnp.pad(w.astype(BF16), ((0, 0), (0, 0), (0, pad)))

    w_up = jnp.pad(w_alpha_up, ((0, 0), (0, RANK_PAD - GATE_RANK), (0, 0)))
    params = (
        packed(w_in[:, :, :A_COLS], OFF_AL + RANK_PAD), packed(w_in[:, :, A_COLS:]), w_up.astype(BF16),
        b_alpha.reshape(depth, 1, DK), gla_norm_g.reshape(depth, 1, DV),
        w_pool_grp.astype(BF16), pool_scale.reshape(depth, 1, D_MODEL),
        b_merge.reshape(depth, 1, 2 * D_MODEL),
        packed(w_proj_a), packed(w_proj_b), packed(w_out),
        ln_g.reshape(depth, 1, D_MODEL), ln_b.reshape(depth, 1, D_MODEL))
    for layer in range(depth):
        x = _layer_call(x, layer, params, consts, SEQ_TILE)
    return x
```

```python
import functools

import numpy as np
import jax
import jax.numpy as jnp
from jax import lax
from jax.experimental import pallas as pl
from jax.experimental.pallas import tpu as pltpu

D_MODEL = 1024
DEPTH = 4
CHUNK = 64
HEADS = 4
DK = D_MODEL // 2
DV = D_MODEL
HEAD_DK = DK // HEADS
HEAD_DV = DV // HEADS
GATE_RANK = 16
GATE_TAU = 16.0
POOL_WINDOWS = (2, 4, 8, 16)
POOL_GROUP_DIM = D_MODEL // len(POOL_WINDOWS)
POOL_HIST = 16
DEEPNORM_ALPHA = (2.0 * DEPTH) ** 0.25
EPS = 1e-5

LANE = 128
MXU_COLS = 256
RANK_PAD = LANE
PROJ_BLOCK = MXU_COLS
N_LEVELS = 6
SUBLANES = 8
VEC_LEVELS = 3
N_SPLIT = 3
SEQ_TILE = 256
EPILOGUE_PARTS = 2
REALIGN_ROWS = 128
VMEM_LIMIT_BYTES = 60 * 1024 * 1024

OFF_Q = 0
OFF_K = OFF_Q + DK
OFF_V = OFF_K + DK
OFF_GA = OFF_V + DV
OFF_AL = OFF_GA + DV
A_COLS = OFF_AL + GATE_RANK
OFF_PI = 0
OFF_GB = OFF_PI + D_MODEL
OFF_M = OFF_GB + D_MODEL
B_COLS = OFF_M + 2 * D_MODEL


def _padded_cols(n):
    tiles = -(-n // LANE)
    return (tiles + 1 - tiles % 2) * LANE

F32 = jnp.float32
BF16 = jnp.bfloat16


def _decay_tables():
    c = CHUNK
    t = np.zeros((N_LEVELS + 1, c, c), np.float32)
    masks = np.zeros((N_LEVELS + 1, c, c), np.float32)
    for l in range(N_LEVELS):
        n = c >> l
        half = n // 2
        for r in range(c):
            base = (r // n) * n
            b = base + half - 1
            if r > b:
                t[l, r, b + 1:r + 1] = 1.0
            else:
                t[l, r, r + 1:b + 1] = 1.0
            for s in range(base, base + n):
                if (r > b) != (s > b):
                    masks[l, r, s] = 1.0
    for r in range(c):
        t[N_LEVELS, r, :r + 1] = 1.0
    masks[N_LEVELS] = np.eye(c, dtype=np.float32)
    tmat = np.concatenate([t[VEC_LEVELS:N_LEVELS - 1], t[N_LEVELS:]]).reshape(-1, c)
    tmat = np.concatenate([tmat] * N_SPLIT, axis=1)
    return tmat, masks


def _pool_tables():
    inv_w = np.zeros((1, D_MODEL), np.float32)
    inv_head = np.zeros((POOL_HIST, D_MODEL), np.float32)
    for g, w in enumerate(POOL_WINDOWS):
        cols = slice(g * POOL_GROUP_DIM, (g + 1) * POOL_GROUP_DIM)
        inv_w[0, cols] = 1.0 / w
        for r in range(POOL_HIST):
            inv_head[r, cols] = 1.0 / min(r + 1, w)
    return inv_w, inv_head


def _dot(a, b):
    return jnp.dot(a, b, preferred_element_type=F32)


def _dot_tn(a, b):
    return lax.dot_general(a, b, (((0,), (0,)), ((), ())), preferred_element_type=F32)


def _block_exponents(g, level):
    n = CHUNK >> level
    half = n // 2
    assert half % SUBLANES == 0
    parts = []
    for base in range(0, CHUNK, n):
        g_b = g[base + half - 1:base + half, :]
        parts.append(g_b - g[base:base + half, :])
        parts.append(g[base + half:base + n, :] - g_b)
    return jnp.concatenate(parts, axis=0)


def _sigmoid(x):
    return 1.0 / (1.0 + jnp.exp(-x))


def _layer_kernel(x_ref, wa_ref, wup_ref, balpha_ref, gnorm_ref, wgrp_ref,
                  pscale_ref, bmerge_ref, wpa_ref, wpb_ref, wout_ref, lng_ref, lnb_ref,
                  tmat_ref, masks_ref, invw_ref, invhead_ref,
                  out_ref,
                  q_s, k_s, v_s, o_s, state_s, pbuf_s, pooled_s,
                  gsilu_s, ga_s, gbzb_s, xprev_s, wb_ref, *, tiles_per_seq):
    tile = x_ref.shape[1]
    n_chunks = tile // CHUNK
    step = pl.program_id(0)
    j = lax.rem(step, tiles_per_seq)

    @pl.when(step == 0)
    def _():
        for ref in (o_s, gsilu_s, ga_s, gbzb_s, xprev_s):
            ref[...] = jnp.zeros_like(ref)
        for r in range(0, D_MODEL, REALIGN_ROWS):
            wb_ref[r:r + REALIGN_ROWS, :B_COLS] = wa_ref[r:r + REALIGN_ROWS, A_COLS:A_COLS + B_COLS]

    @pl.when(j == 0)
    def _():
        state_s[...] = jnp.zeros_like(state_s)
        pbuf_s[0:POOL_HIST, :] = jnp.zeros((POOL_HIST, D_MODEL), F32)

    x = x_ref[0]
    xb = x.astype(BF16)
    halves = [slice(i * (tile // EPILOGUE_PARTS), (i + 1) * (tile // EPILOGUE_PARTS))
              for i in range(EPILOGUE_PARTS)]

    def branch_a(rs):
        ya_parts = []
        for h in range(HEADS):
            vl = slice(h * HEAD_DV, (h + 1) * HEAD_DV)
            o_h = o_s[rs, vl]
            ms = jnp.mean(o_h * o_h, axis=-1, keepdims=True)
            ya_parts.append(o_h * lax.rsqrt(ms + EPS) * gnorm_ref[:, vl])
        y_a = (jnp.concatenate(ya_parts, axis=1) * gsilu_s[rs, :]).astype(BF16)
        return _dot(y_a, wpa_ref[:, :D_MODEL])

    def merge_out(rs, z_a):
        merged = (ga_s[rs, :] * z_a + gbzb_s[rs, :]).astype(BF16)
        return _dot(merged, wout_ref[:, :D_MODEL])

    def post_norm(rs, y):
        r = DEEPNORM_ALPHA * xprev_s[rs, :] + y
        mu = jnp.mean(r, axis=-1, keepdims=True)
        d = r - mu
        var = jnp.mean(d * d, axis=-1, keepdims=True)
        out_ref[0, rs, :] = d * lax.rsqrt(var + EPS) * lng_ref[...] + lnb_ref[...]

    def proj(w_ref, col0, ncols):
        return [lambda c0=c0: _dot(xb, w_ref[:, c0:c0 + PROJ_BLOCK])
                for c0 in range(col0, col0 + ncols, PROJ_BLOCK)]

    def cat(parts):
        return jnp.concatenate(parts, axis=1)

    qk = _dot(xb, wa_ref[:, OFF_Q:OFF_V])
    q_s[...] = qk[:, :DK] * (HEAD_DK ** -0.5)
    k_s[...] = qk[:, DK:]
    v_s[...] = _dot(xb, wa_ref[:, OFF_V:OFF_GA]).astype(BF16)
    al = _dot(xb, wa_ref[:, OFF_AL:OFF_AL + RANK_PAD])
    z_as = [branch_a(rs) for rs in halves]
    u = cat([f() for f in proj(wb_ref, OFF_PI, D_MODEL)])
    z = _dot(al.astype(BF16), wup_ref[...]) + balpha_ref[...]
    gate_b = cat([f() for f in proj(wb_ref, OFF_GB, D_MODEL)])
    la = (jnp.minimum(z, 0.0) - jnp.log1p(jnp.exp(-jnp.abs(z)))) * (1.0 / GATE_TAU)
    silu_b = gate_b * _sigmoid(gate_b)
    ys = [merge_out(rs, z_a) for rs, z_a in zip(halves, z_as)]
    gate_a = cat([f() for f in proj(wa_ref, OFF_GA, D_MODEL)])
    gsilu_s[...] = gate_a * _sigmoid(gate_a)

    n_fine = N_LEVELS - 1 - VEC_LEVELS
    odd_row = (lax.broadcasted_iota(jnp.int32, (CHUNK, DK), 0) & 1) == 1
    decs, e_gs, e_rs = [], [], []
    for c in range(n_chunks):
        la_c = la[c * CHUNK:(c + 1) * CHUNK, :]
        p0 = la_c.astype(BF16)
        rem = la_c - p0.astype(F32)
        p1 = rem.astype(BF16)
        p2 = (rem - p1.astype(F32)).astype(BF16)
        pieces = jnp.concatenate([p0, p1, p2], axis=0)
        expo = _dot(tmat_ref[...], pieces)
        g = expo[n_fine * CHUNK:, :]
        fine = jnp.exp(expo)
        pairs = jnp.where(odd_row, jnp.exp(la_c), 1.0)
        decs.append([jnp.exp(_block_exponents(g, l)) for l in range(VEC_LEVELS)]
                    + [fine[i * CHUNK:(i + 1) * CHUNK, :] for i in range(n_fine)] + [pairs])
        e_gs.append(fine[n_fine * CHUNK:, :])
        e_rs.append(jnp.exp(g[CHUNK - 1:CHUNK, :] - g))
    merge_blocks = proj(wb_ref, OFF_M, 2 * D_MODEL)
    merge_a = cat([f() for f in merge_blocks[:4]])
    for rs, y in zip(halves, ys):
        post_norm(rs, y)
    xprev_s[...] = x
    ga_s[...] = _sigmoid(merge_a + bmerge_ref[:, :D_MODEL])

    pbuf_s[POOL_HIST:POOL_HIST + tile, :] = u
    for g, w in enumerate(POOL_WINDOWS):
        cols = slice(g * POOL_GROUP_DIM, (g + 1) * POOL_GROUP_DIM)
        s = pbuf_s[:, cols]
        shift = 1
        while shift < w:
            s = s + pltpu.roll(s, shift, 0)
            shift *= 2
        s = s[POOL_HIST:POOL_HIST + tile, :]
        inv_head = jnp.where(j == 0, invhead_ref[:, cols], invw_ref[:, cols])
        pooled_s[0:POOL_HIST, cols] = (s[0:POOL_HIST, :] * inv_head
                                       - u[0:POOL_HIST, cols]).astype(BF16)
        pooled_s[POOL_HIST:, cols] = (s[POOL_HIST:, :] * invw_ref[:, cols]
                                      - u[POOL_HIST:, cols]).astype(BF16)
    pbuf_s[0:POOL_HIST, :] = pbuf_s[tile:tile + POOL_HIST, :]

    qgs, krs, a_lasts, scores = [], [], [], []
    for c in range(n_chunks):
        rows = slice(c * CHUNK, (c + 1) * CHUNK)
        dec = decs[c]
        q_c = q_s[rows, :]
        k_c = k_s[rows, :]
        qgs.append((q_c * e_gs[c]).astype(BF16))
        krs.append((k_c * e_rs[c]).astype(BF16))
        a_lasts.append(e_gs[c][CHUNK - 1:CHUNK, :])
        sc_c = []
        q_cb = q_c.astype(BF16)
        k_cb = k_c.astype(BF16)
        dec_b = [d.astype(BF16) for d in dec]
        for h in range(HEADS):
            lanes = slice(h * HEAD_DK, (h + 1) * HEAD_DK)
            acc = jnp.zeros((CHUNK, CHUNK), F32)
            for l in range(N_LEVELS + 1):
                if l < N_LEVELS:
                    a_l = q_cb[:, lanes] * dec_b[l][:, lanes]
                    b_l = k_cb[:, lanes] * dec_b[l][:, lanes]
                else:
                    a_l = q_cb[:, lanes]
                    b_l = k_cb[:, lanes]
                acc = acc + _dot(a_l, b_l.T) * masks_ref[l]
            sc_c.append(acc.astype(BF16))
        scores.append(sc_c)
    merge_b = [f() for f in merge_blocks[4:6]]
    mixed = cat([_dot(pooled_s[:, g * POOL_GROUP_DIM:(g + 1) * POOL_GROUP_DIM], wgrp_ref[g])
                 for g in range(len(POOL_WINDOWS))])
    merge_b += [f() for f in merge_blocks[6:]]
    y_b = (mixed * pscale_ref[...] * silu_b).astype(BF16)
    gate_mb = _sigmoid(cat(merge_b) + bmerge_ref[:, D_MODEL:])

    zb_blocks = [lambda c0=c0: _dot(y_b, wpb_ref[:, c0:c0 + PROJ_BLOCK])
                 for c0 in range(0, D_MODEL, PROJ_BLOCK)]
    zb_parts = []
    for c in range(n_chunks):
        rows = slice(c * CHUNK, (c + 1) * CHUNK)
        sts = [state_s[h] for h in range(HEADS)]
        for h in range(HEADS):
            kl = slice(h * HEAD_DK, (h + 1) * HEAD_DK)
            vl = slice(h * HEAD_DV, (h + 1) * HEAD_DV)
            lhs = jnp.concatenate([qgs[c][:, kl], scores[c][h]], axis=1)
            rhs = jnp.concatenate([sts[h].astype(BF16), v_s[rows, vl]], axis=0)
            o_s[rows, vl] = _dot(lhs, rhs)
        for h in range(HEADS):
            kl = slice(h * HEAD_DK, (h + 1) * HEAD_DK)
            vl = slice(h * HEAD_DV, (h + 1) * HEAD_DV)
            a_col = jnp.broadcast_to(a_lasts[c][:, kl], (HEAD_DK, HEAD_DK)).T
            a_col = jnp.concatenate([a_col, a_col], axis=1)
            state_s[h] = sts[h] * a_col + _dot_tn(krs[c][:, kl], v_s[rows, vl])
        n_fill = (len(zb_blocks) * (c + 1)) // n_chunks - len(zb_parts)
        zb_parts += [zb_blocks[len(zb_parts) + i]() for i in range(n_fill)]
    z_b = cat(zb_parts)

    gbzb_s[...] = gate_mb * z_b


def _const_spec(shape):
    zeros = (0,) * len(shape)
    return pl.BlockSpec(shape, lambda s: zeros, pipeline_mode=pl.Buffered(1))


def _layer_spec(shape, layer):
    zeros = (0,) * (len(shape) - 1)
    return pl.BlockSpec((None,) + tuple(shape[1:]), lambda s: (layer,) + zeros,
                        pipeline_mode=pl.Buffered(1))


def _layer_call(x, layer, params, consts, tile):
    batch, seq, _ = x.shape
    tiles_per_seq = seq // tile
    n_tiles = batch * tiles_per_seq

    def tile_index(t):
        return (t // tiles_per_seq, t % tiles_per_seq, 0)

    in_specs = [pl.BlockSpec((1, tile, D_MODEL), lambda s: tile_index(jnp.minimum(s, n_tiles - 1)))]
    in_specs += [_layer_spec(a.shape, layer) for a in params]
    in_specs += [_const_spec(a.shape) for a in consts]
    return pl.pallas_call(
        functools.partial(_layer_kernel, tiles_per_seq=tiles_per_seq),
        out_shape=jax.ShapeDtypeStruct(x.shape, x.dtype),
        grid=(n_tiles + 1,),
        in_specs=in_specs,
        out_specs=pl.BlockSpec((1, tile, D_MODEL), lambda s: tile_index(jnp.maximum(s - 1, 0))),
        scratch_shapes=[
            pltpu.VMEM((tile, DK), F32),
            pltpu.VMEM((tile, DK), F32),
            pltpu.VMEM((tile, DV), BF16),
            pltpu.VMEM((tile, DV), F32),
            pltpu.VMEM((HEADS, HEAD_DK, HEAD_DV), F32),
            pltpu.VMEM((tile + POOL_HIST, D_MODEL), F32),
            pltpu.VMEM((tile, D_MODEL), BF16),
            pltpu.VMEM((tile, D_MODEL), F32),
            pltpu.VMEM((tile, D_MODEL), F32),
            pltpu.VMEM((tile, D_MODEL), F32),
            pltpu.VMEM((tile, D_MODEL), F32),
            pltpu.VMEM((D_MODEL, _padded_cols(B_COLS)), BF16),
        ],
        compiler_params=pltpu.CompilerParams(
            dimension_semantics=("arbitrary",),
            vmem_limit_bytes=VMEM_LIMIT_BYTES),
        name="gla_pool_layer",
    )(x, *params, *consts)


def kernel(x, w_in, w_alpha_up, b_alpha, gla_norm_g, w_pool_grp, pool_scale, b_merge,
           w_proj_a, w_proj_b, w_out, ln_g, ln_b):
    batch, seq, d = x.shape
    depth = w_in.shape[0]
    assert d == D_MODEL and seq % SEQ_TILE == 0 and depth == DEPTH
    tmat, masks = _decay_tables()
    inv_w, inv_head = _pool_tables()
    consts = (jnp.asarray(tmat, BF16), jnp.asarray(masks, F32),
              jnp.asarray(inv_w, F32), jnp.asarray(inv_head, F32))
    def packed(w):
        pad = _padded_cols(w.shape[-1]) - w.shape[-1]
        return jnp.pad(w.astype(BF16), ((0, 0), (0, 0), (0, pad)))

    assert _padded_cols(w_in.shape[-1]) == -(-w_in.shape[-1] // LANE) * LANE
    w_up = jnp.pad(w_alpha_up, ((0, 0), (0, RANK_PAD - GATE_RANK), (0, 0)))
    params = (
        w_in.astype(BF16), w_up.astype(BF16),
        b_alpha.reshape(depth, 1, DK), gla_norm_g.reshape(depth, 1, DV),
        w_pool_grp.astype(BF16), pool_scale.reshape(depth, 1, D_MODEL),
        b_merge.reshape(depth, 1, 2 * D_MODEL),
        packed(w_proj_a), packed(w_proj_b), packed(w_out),
        ln_g.reshape(depth, 1, D_MODEL), ln_b.reshape(depth, 1, D_MODEL))
    for layer in range(depth):
        x = _layer_call(x, layer, params, consts, SEQ_TILE)
    return x
```

```python
import functools

import numpy as np
import jax
import jax.numpy as jnp
from jax import lax
from jax.experimental import pallas as pl
from jax.experimental.pallas import tpu as pltpu

D_MODEL = 1024
DEPTH = 4
CHUNK = 64
HEADS = 4
DK = D_MODEL // 2
DV = D_MODEL
HEAD_DK = DK // HEADS
HEAD_DV = DV // HEADS
GATE_RANK = 16
GATE_TAU = 16.0
POOL_WINDOWS = (2, 4, 8, 16)
POOL_GROUP_DIM = D_MODEL // len(POOL_WINDOWS)
POOL_HIST = 16
DEEPNORM_ALPHA = (2.0 * DEPTH) ** 0.25
EPS = 1e-5

LANE = 128
MXU_COLS = 256
RANK_PAD = LANE
PROJ_BLOCK = MXU_COLS
N_LEVELS = 6
SUBLANES = 8
VEC_LEVELS = 3
N_SPLIT = 3
SEQ_TILE = 256
EPILOGUE_PARTS = 2
REALIGN_ROWS = 128
VMEM_LIMIT_BYTES = 60 * 1024 * 1024

OFF_Q = 0
OFF_K = OFF_Q + DK
OFF_V = OFF_K + DK
OFF_GA = OFF_V + DV
OFF_AL = OFF_GA + DV
A_COLS = OFF_AL + GATE_RANK
OFF_PI = 0
OFF_GB = OFF_PI + D_MODEL
OFF_M = OFF_GB + D_MODEL
B_COLS = OFF_M + 2 * D_MODEL


def _padded_cols(n):
    tiles = -(-n // LANE)
    return (tiles + 1 - tiles % 2) * LANE

F32 = jnp.float32
BF16 = jnp.bfloat16


def _decay_tables():
    c = CHUNK
    t = np.zeros((N_LEVELS + 1, c, c), np.float32)
    masks = np.zeros((N_LEVELS + 1, c, c), np.float32)
    for l in range(N_LEVELS):
        n = c >> l
        half = n // 2
        for r in range(c):
            base = (r // n) * n
            b = base + half - 1
            if r > b:
                t[l, r, b + 1:r + 1] = 1.0
            else:
                t[l, r, r + 1:b + 1] = 1.0
            for s in range(base, base + n):
                if (r > b) != (s > b):
                    masks[l, r, s] = 1.0
    for r in range(c):
        t[N_LEVELS, r, :r + 1] = 1.0
    masks[N_LEVELS] = np.eye(c, dtype=np.float32)
    tmat = np.concatenate([t[VEC_LEVELS:N_LEVELS - 1], t[N_LEVELS:]]).reshape(-1, c)
    tmat = np.concatenate([tmat] * N_SPLIT, axis=1)
    return tmat, masks


def _pool_tables():
    inv_w = np.zeros((1, D_MODEL), np.float32)
    inv_head = np.zeros((POOL_HIST, D_MODEL), np.float32)
    for g, w in enumerate(POOL_WINDOWS):
        cols = slice(g * POOL_GROUP_DIM, (g + 1) * POOL_GROUP_DIM)
        inv_w[0, cols] = 1.0 / w
        for r in range(POOL_HIST):
            inv_head[r, cols] = 1.0 / min(r + 1, w)
    return inv_w, inv_head


def _dot(a, b):
    return jnp.dot(a, b, preferred_element_type=F32)


def _dot_tn(a, b):
    return lax.dot_general(a, b, (((0,), (0,)), ((), ())), preferred_element_type=F32)


def _block_exponents(g, level):
    n = CHUNK >> level
    half = n // 2
    assert half % SUBLANES == 0
    parts = []
    for base in range(0, CHUNK, n):
        g_b = g[base + half - 1:base + half, :]
        parts.append(g_b - g[base:base + half, :])
        parts.append(g[base + half:base + n, :] - g_b)
    return jnp.concatenate(parts, axis=0)


def _sigmoid(x):
    return 1.0 / (1.0 + jnp.exp(-x))


def _layer_kernel(x_ref, wa_ref, wup_ref, balpha_ref, gnorm_ref, wgrp_ref,
                  pscale_ref, bmerge_ref, wpa_ref, wpb_ref, wout_ref, lng_ref, lnb_ref,
                  tmat_ref, masks_ref, invw_ref, invhead_ref,
                  out_ref,
                  q_s, k_s, v_s, o_s, state_s, pbuf_s, pooled_s,
                  gsilu_s, ga_s, gbzb_s, xprev_s, wb_ref, *, tiles_per_seq):
    tile = x_ref.shape[1]
    n_chunks = tile // CHUNK
    step = pl.program_id(0)
    j = lax.rem(step, tiles_per_seq)

    @pl.when(step == 0)
    def _():
        for ref in (o_s, gsilu_s, ga_s, gbzb_s, xprev_s):
            ref[...] = jnp.zeros_like(ref)
        for r in range(0, D_MODEL, REALIGN_ROWS):
            wb_ref[r:r + REALIGN_ROWS, :B_COLS] = wa_ref[r:r + REALIGN_ROWS, A_COLS:A_COLS + B_COLS]

    @pl.when(j == 0)
    def _():
        state_s[...] = jnp.zeros_like(state_s)
        pbuf_s[0:POOL_HIST, :] = jnp.zeros((POOL_HIST, D_MODEL), F32)

    x = x_ref[0]
    xb = x.astype(BF16)
    halves = [slice(i * (tile // EPILOGUE_PARTS), (i + 1) * (tile // EPILOGUE_PARTS))
              for i in range(EPILOGUE_PARTS)]

    def branch_a(rs):
        ya_parts = []
        for h in range(HEADS):
            vl = slice(h * HEAD_DV, (h + 1) * HEAD_DV)
            o_h = o_s[rs, vl]
            ms = jnp.mean(o_h * o_h, axis=-1, keepdims=True)
            ya_parts.append(o_h * lax.rsqrt(ms + EPS) * gnorm_ref[:, vl])
        y_a = (jnp.concatenate(ya_parts, axis=1) * gsilu_s[rs, :]).astype(BF16)
        return _dot(y_a, wpa_ref[:, :D_MODEL])

    def merge_out(rs, z_a):
        merged = (ga_s[rs, :] * z_a + gbzb_s[rs, :]).astype(BF16)
        return _dot(merged, wout_ref[:, :D_MODEL])

    def post_norm(rs, y):
        r = DEEPNORM_ALPHA * xprev_s[rs, :] + y
        mu = jnp.mean(r, axis=-1, keepdims=True)
        d = r - mu
        var = jnp.mean(d * d, axis=-1, keepdims=True)
        out_ref[0, rs, :] = d * lax.rsqrt(var + EPS) * lng_ref[...] + lnb_ref[...]

    def proj(w_ref, col0, ncols):
        return [lambda c0=c0: _dot(xb, w_ref[:, c0:c0 + PROJ_BLOCK])
                for c0 in range(col0, col0 + ncols, PROJ_BLOCK)]

    def cat(parts):
        return jnp.concatenate(parts, axis=1)

    qk = _dot(xb, wa_ref[:, OFF_Q:OFF_V])
    q_s[...] = qk[:, :DK] * (HEAD_DK ** -0.5)
    k_s[...] = qk[:, DK:]
    v_s[...] = _dot(xb, wa_ref[:, OFF_V:OFF_GA]).astype(BF16)
    al = _dot(xb, wa_ref[:, OFF_AL:OFF_AL + RANK_PAD])
    z_as = [branch_a(rs) for rs in halves]
    u = cat([f() for f in proj(wb_ref, OFF_PI, D_MODEL)])
    z = _dot(al.astype(BF16), wup_ref[...]) + balpha_ref[...]
    gate_b = cat([f() for f in proj(wb_ref, OFF_GB, D_MODEL)])
    la = (jnp.minimum(z, 0.0) - jnp.log(1.0 + jnp.exp(-jnp.abs(z)))) * (1.0 / GATE_TAU)
    silu_b = gate_b * _sigmoid(gate_b)
    ys = [merge_out(rs, z_a) for rs, z_a in zip(halves, z_as)]
    gate_a = cat([f() for f in proj(wa_ref, OFF_GA, D_MODEL)])
    gsilu_s[...] = gate_a * _sigmoid(gate_a)

    n_fine = N_LEVELS - 1 - VEC_LEVELS
    odd_row = (lax.broadcasted_iota(jnp.int32, (CHUNK, DK), 0) & 1) == 1
    decs, e_gs, e_rs = [], [], []
    for c in range(n_chunks):
        la_c = la[c * CHUNK:(c + 1) * CHUNK, :]
        p0 = la_c.astype(BF16)
        rem = la_c - p0.astype(F32)
        p1 = rem.astype(BF16)
        p2 = (rem - p1.astype(F32)).astype(BF16)
        pieces = jnp.concatenate([p0, p1, p2], axis=0)
        expo = _dot(tmat_ref[...], pieces)
        g = expo[n_fine * CHUNK:, :]
        fine = jnp.exp(expo)
        pairs = jnp.where(odd_row, jnp.exp(la_c), 1.0)
        decs.append([jnp.exp(_block_exponents(g, l)) for l in range(VEC_LEVELS)]
                    + [fine[i * CHUNK:(i + 1) * CHUNK, :] for i in range(n_fine)] + [pairs])
        e_gs.append(fine[n_fine * CHUNK:, :])
        e_rs.append(jnp.exp(g[CHUNK - 1:CHUNK, :] - g))
    merge_blocks = proj(wb_ref, OFF_M, 2 * D_MODEL)
    merge_a = cat([f() for f in merge_blocks[:4]])
    for rs, y in zip(halves, ys):
        post_norm(rs, y)
    xprev_s[...] = x
    ga_s[...] = _sigmoid(merge_a + bmerge_ref[:, :D_MODEL])

    pbuf_s[POOL_HIST:POOL_HIST + tile, :] = u
    for g, w in enumerate(POOL_WINDOWS):
        cols = slice(g * POOL_GROUP_DIM, (g + 1) * POOL_GROUP_DIM)
        s = pbuf_s[:, cols]
        shift = 1
        while shift < w:
            s = s + pltpu.roll(s, shift, 0)
            shift *= 2
        s = s[POOL_HIST:POOL_HIST + tile, :]
        inv_head = jnp.where(j == 0, invhead_ref[:, cols], invw_ref[:, cols])
        pooled_s[0:POOL_HIST, cols] = (s[0:POOL_HIST, :] * inv_head
                                       - u[0:POOL_HIST, cols]).astype(BF16)
        pooled_s[POOL_HIST:, cols] = (s[POOL_HIST:, :] * invw_ref[:, cols]
                                      - u[POOL_HIST:, cols]).astype(BF16)
    pbuf_s[0:POOL_HIST, :] = pbuf_s[tile:tile + POOL_HIST, :]

    qgs, krs, a_lasts, scores = [], [], [], []
    for c in range(n_chunks):
        rows = slice(c * CHUNK, (c + 1) * CHUNK)
        dec = decs[c]
        q_c = q_s[rows, :]
        k_c = k_s[rows, :]
        qgs.append((q_c * e_gs[c]).astype(BF16))
        krs.append((k_c * e_rs[c]).astype(BF16))
        a_lasts.append(e_gs[c][CHUNK - 1:CHUNK, :])
        sc_c = []
        q_cb = q_c.astype(BF16)
        k_cb = k_c.astype(BF16)
        dec_b = [d.astype(BF16) for d in dec]
        for h in range(HEADS):
            lanes = slice(h * HEAD_DK, (h + 1) * HEAD_DK)
            acc = jnp.zeros((CHUNK, CHUNK), F32)
            for l in range(N_LEVELS + 1):
                if l < N_LEVELS:
                    a_l = q_cb[:, lanes] * dec_b[l][:, lanes]
                    b_l = k_cb[:, lanes] * dec_b[l][:, lanes]
                else:
                    a_l = q_cb[:, lanes]
                    b_l = k_cb[:, lanes]
                acc = acc + _dot(a_l, b_l.T) * masks_ref[l]
            sc_c.append(acc.astype(BF16))
        scores.append(sc_c)
    merge_b = [f() for f in merge_blocks[4:6]]
    mixed = cat([_dot(pooled_s[:, g * POOL_GROUP_DIM:(g + 1) * POOL_GROUP_DIM], wgrp_ref[g])
                 for g in range(len(POOL_WINDOWS))])
    merge_b += [f() for f in merge_blocks[6:]]
    y_b = (mixed * pscale_ref[...] * silu_b).astype(BF16)
    gate_mb = _sigmoid(cat(merge_b) + bmerge_ref[:, D_MODEL:])

    zb_blocks = [lambda c0=c0: _dot(y_b, wpb_ref[:, c0:c0 + PROJ_BLOCK])
                 for c0 in range(0, D_MODEL, PROJ_BLOCK)]
    zb_parts = []
    for c in range(n_chunks):
        rows = slice(c * CHUNK, (c + 1) * CHUNK)
        sts = [state_s[h] for h in range(HEADS)]
        for h in range(HEADS):
            kl = slice(h * HEAD_DK, (h + 1) * HEAD_DK)
            vl = slice(h * HEAD_DV, (h + 1) * HEAD_DV)
            lhs = jnp.concatenate([qgs[c][:, kl], scores[c][h]], axis=1)
            rhs = jnp.concatenate([sts[h].astype(BF16), v_s[rows, vl]], axis=0)
            o_s[rows, vl] = _dot(lhs, rhs)
        for h in range(HEADS):
            kl = slice(h * HEAD_DK, (h + 1) * HEAD_DK)
            vl = slice(h * HEAD_DV, (h + 1) * HEAD_DV)
            a_col = jnp.broadcast_to(a_lasts[c][:, kl], (HEAD_DK, HEAD_DK)).T
            a_col = jnp.concatenate([a_col, a_col], axis=1)
            state_s[h] = sts[h] * a_col + _dot_tn(krs[c][:, kl], v_s[rows, vl])
        n_fill = min(len(zb_blocks), -(-len(zb_blocks) * (c + 1) // (n_chunks - 1))) - len(zb_parts)
        zb_parts += [zb_blocks[len(zb_parts) + i]() for i in range(n_fill)]
    z_b = cat(zb_parts)

    gbzb_s[...] = gate_mb * z_b


def _const_spec(shape):
    zeros = (0,) * len(shape)
    return pl.BlockSpec(shape, lambda s: zeros, pipeline_mode=pl.Buffered(1))


def _layer_spec(shape, layer):
    zeros = (0,) * (len(shape) - 1)
    return pl.BlockSpec((None,) + tuple(shape[1:]), lambda s: (layer,) + zeros,
                        pipeline_mode=pl.Buffered(1))


def _layer_call(x, layer, params, consts, tile):
    batch, seq, _ = x.shape
    tiles_per_seq = seq // tile
    n_tiles = batch * tiles_per_seq

    def tile_index(t):
        return (t // tiles_per_seq, t % tiles_per_seq, 0)

    in_specs = [pl.BlockSpec((1, tile, D_MODEL), lambda s: tile_index(jnp.minimum(s, n_tiles - 1)))]
    in_specs += [_layer_spec(a.shape, layer) for a in params]
    in_specs += [_const_spec(a.shape) for a in consts]
    return pl.pallas_call(
        functools.partial(_layer_kernel, tiles_per_seq=tiles_per_seq),
        out_shape=jax.ShapeDtypeStruct(x.shape, x.dtype),
        grid=(n_tiles + 1,),
        in_specs=in_specs,
        out_specs=pl.BlockSpec((1, tile, D_MODEL), lambda s: tile_index(jnp.maximum(s - 1, 0))),
        scratch_shapes=[
            pltpu.VMEM((tile, DK), F32),
            pltpu.VMEM((tile, DK), F32),
            pltpu.VMEM((tile, DV), BF16),
            pltpu.VMEM((tile, DV), F32),
            pltpu.VMEM((HEADS, HEAD_DK, HEAD_DV), F32),
            pltpu.VMEM((tile + POOL_HIST, D_MODEL), F32),
            pltpu.VMEM((tile, D_MODEL), BF16),
            pltpu.VMEM((tile, D_MODEL), F32),
            pltpu.VMEM((tile, D_MODEL), F32),
            pltpu.VMEM((tile, D_MODEL), F32),
            pltpu.VMEM((tile, D_MODEL), F32),
            pltpu.VMEM((D_MODEL, _padded_cols(B_COLS)), BF16),
        ],
        compiler_params=pltpu.CompilerParams(
            dimension_semantics=("arbitrary",),
            vmem_limit_bytes=VMEM_LIMIT_BYTES),
        name="gla_pool_layer",
    )(x, *params, *consts)


def kernel(x, w_in, w_alpha_up, b_alpha, gla_norm_g, w_pool_grp, pool_scale, b_merge,
           w_proj_a, w_proj_b, w_out, ln_g, ln_b):
    batch, seq, d = x.shape
    depth = w_in.shape[0]
    assert d == D_MODEL and seq % SEQ_TILE == 0 and depth == DEPTH
    tmat, masks = _decay_tables()
    inv_w, inv_head = _pool_tables()
    consts = (jnp.asarray(tmat, BF16), jnp.asarray(masks, F32),
              jnp.asarray(inv_w, F32), jnp.asarray(inv_head, F32))
    def packed(w):
        pad = _padded_cols(w.shape[-1]) - w.shape[-1]
        return jnp.pad(w.astype(BF16), ((0, 0), (0, 0), (0, pad)))

    assert _padded_cols(w_in.shape[-1]) == -(-w_in.shape[-1] // LANE) * LANE
    w_up = jnp.pad(w_alpha_up, ((0, 0), (0, RANK_PAD - GATE_RANK), (0, 0)))
    params = (
        w_in.astype(BF16), w_up.astype(BF16),
        b_alpha.reshape(depth, 1, DK), gla_norm_g.reshape(depth, 1, DV),
        w_pool_grp.astype(BF16), pool_scale.reshape(depth, 1, D_MODEL),
        b_merge.reshape(depth, 1, 2 * D_MODEL),
        packed(w_proj_a), packed(w_proj_b), packed(w_out),
        ln_g.reshape(depth, 1, D_MODEL), ln_b.reshape(depth, 1, D_MODEL))
    for layer in range(depth):
        x = _layer_call(x, layer, params, consts, SEQ_TILE)
    return x
```

```python
import functools

import numpy as np
import jax
import jax.numpy as jnp
from jax import lax
from jax.experimental import pallas as pl
from jax.experimental.pallas import tpu as pltpu

D_MODEL = 1024
DEPTH = 4
CHUNK = 64
HEADS = 4
DK = D_MODEL // 2
DV = D_MODEL
HEAD_DK = DK // HEADS
HEAD_DV = DV // HEADS
GATE_RANK = 16
GATE_TAU = 16.0
POOL_WINDOWS = (2, 4, 8, 16)
POOL_GROUP_DIM = D_MODEL // len(POOL_WINDOWS)
POOL_HIST = 16
DEEPNORM_ALPHA = (2.0 * DEPTH) ** 0.25
EPS = 1e-5

LANE = 128
MXU_COLS = 256
RANK_PAD = LANE
PROJ_BLOCK = MXU_COLS
N_LEVELS = 6
SUBLANES = 8
VEC_LEVELS = 3
N_SPLIT = 3
SEQ_TILE = 256
EPILOGUE_PARTS = 1
REALIGN_ROWS = 128
VMEM_LIMIT_BYTES = 60 * 1024 * 1024

OFF_Q = 0
OFF_K = OFF_Q + DK
OFF_V = OFF_K + DK
OFF_GA = OFF_V + DV
OFF_AL = OFF_GA + DV
A_COLS = OFF_AL + GATE_RANK
OFF_PI = 0
OFF_GB = OFF_PI + D_MODEL
OFF_M = OFF_GB + D_MODEL
B_COLS = OFF_M + 2 * D_MODEL


def _padded_cols(n):
    tiles = -(-n // LANE)
    return (tiles + 1 - tiles % 2) * LANE

F32 = jnp.float32
BF16 = jnp.bfloat16


def _decay_tables():
    c = CHUNK
    t = np.zeros((N_LEVELS + 1, c, c), np.float32)
    masks = np.zeros((N_LEVELS + 1, c, c), np.float32)
    for l in range(N_LEVELS):
        n = c >> l
        half = n // 2
        for r in range(c):
            base = (r // n) * n
            b = base + half - 1
            if r > b:
                t[l, r, b + 1:r + 1] = 1.0
            else:
                t[l, r, r + 1:b + 1] = 1.0
            for s in range(base, base + n):
                if (r > b) != (s > b):
                    masks[l, r, s] = 1.0
    for r in range(c):
        t[N_LEVELS, r, :r + 1] = 1.0
    masks[N_LEVELS] = np.eye(c, dtype=np.float32)
    tmat = np.concatenate([t[VEC_LEVELS:N_LEVELS - 1], t[N_LEVELS:]]).reshape(-1, c)
    tmat = np.concatenate([tmat] * N_SPLIT, axis=1)
    return tmat, masks


def _pool_tables():
    inv_w = np.zeros((1, D_MODEL), np.float32)
    inv_head = np.zeros((POOL_HIST, D_MODEL), np.float32)
    for g, w in enumerate(POOL_WINDOWS):
        cols = slice(g * POOL_GROUP_DIM, (g + 1) * POOL_GROUP_DIM)
        inv_w[0, cols] = 1.0 / w
        for r in range(POOL_HIST):
            inv_head[r, cols] = 1.0 / min(r + 1, w)
    return inv_w, inv_head


def _dot(a, b):
    return jnp.dot(a, b, preferred_element_type=F32)


def _dot_tn(a, b):
    return lax.dot_general(a, b, (((0,), (0,)), ((), ())), preferred_element_type=F32)


def _block_exponents(g, level):
    n = CHUNK >> level
    half = n // 2
    assert half % SUBLANES == 0
    parts = []
    for base in range(0, CHUNK, n):
        g_b = g[base + half - 1:base + half, :]
        parts.append(g_b - g[base:base + half, :])
        parts.append(g[base + half:base + n, :] - g_b)
    return jnp.concatenate(parts, axis=0)


def _sigmoid(x):
    return 1.0 / (1.0 + jnp.exp(-x))


def _layer_kernel(x_ref, wa_ref, wup_ref, balpha_ref, gnorm_ref, wgrp_ref,
                  pscale_ref, bmerge_ref, wpa_ref, wpb_ref, wout_ref, lng_ref, lnb_ref,
                  tmat_ref, masks_ref, invw_ref, invhead_ref,
                  out_ref,
                  q_s, k_s, v_s, o_s, state_s, pbuf_s, pooled_s,
                  gsilu_s, ga_s, gbzb_s, xprev_s, wb_ref, *, tiles_per_seq):
    tile = x_ref.shape[1]
    n_chunks = tile // CHUNK
    step = pl.program_id(0)
    j = lax.rem(step, tiles_per_seq)

    @pl.when(step == 0)
    def _():
        for ref in (o_s, gsilu_s, ga_s, gbzb_s, xprev_s):
            ref[...] = jnp.zeros_like(ref)
        for r in range(0, D_MODEL, REALIGN_ROWS):
            wb_ref[r:r + REALIGN_ROWS, :B_COLS] = wa_ref[r:r + REALIGN_ROWS, A_COLS:A_COLS + B_COLS]

    @pl.when(j == 0)
    def _():
        state_s[...] = jnp.zeros_like(state_s)
        pbuf_s[0:POOL_HIST, :] = jnp.zeros((POOL_HIST, D_MODEL), F32)

    x = x_ref[0]
    xb = x.astype(BF16)
    halves = [slice(i * (tile // EPILOGUE_PARTS), (i + 1) * (tile // EPILOGUE_PARTS))
              for i in range(EPILOGUE_PARTS)]

    def branch_a(rs):
        ya_parts = []
        for h in range(HEADS):
            vl = slice(h * HEAD_DV, (h + 1) * HEAD_DV)
            o_h = o_s[rs, vl]
            ms = jnp.mean(o_h * o_h, axis=-1, keepdims=True)
            ya_parts.append(o_h * lax.rsqrt(ms + EPS) * gnorm_ref[:, vl])
        y_a = (jnp.concatenate(ya_parts, axis=1) * gsilu_s[rs, :]).astype(BF16)
        return _dot(y_a, wpa_ref[:, :D_MODEL])

    def merge_out(rs, z_a):
        merged = (ga_s[rs, :] * z_a + gbzb_s[rs, :]).astype(BF16)
        return _dot(merged, wout_ref[:, :D_MODEL])

    def post_norm(rs, y):
        r = DEEPNORM_ALPHA * xprev_s[rs, :] + y
        mu = jnp.mean(r, axis=-1, keepdims=True)
        d = r - mu
        var = jnp.mean(d * d, axis=-1, keepdims=True)
        out_ref[0, rs, :] = d * lax.rsqrt(var + EPS) * lng_ref[...] + lnb_ref[...]

    def proj(w_ref, col0, ncols):
        return [lambda c0=c0: _dot(xb, w_ref[:, c0:c0 + PROJ_BLOCK])
                for c0 in range(col0, col0 + ncols, PROJ_BLOCK)]

    def cat(parts):
        return jnp.concatenate(parts, axis=1)

    qk = _dot(xb, wa_ref[:, OFF_Q:OFF_V])
    q_s[...] = qk[:, :DK] * (HEAD_DK ** -0.5)
    k_s[...] = qk[:, DK:]
    v_s[...] = _dot(xb, wa_ref[:, OFF_V:OFF_GA]).astype(BF16)
    al = _dot(xb, wa_ref[:, OFF_AL:OFF_AL + RANK_PAD])
    z_as = [branch_a(rs) for rs in halves]
    u = cat([f() for f in proj(wb_ref, OFF_PI, D_MODEL)])
    z = _dot(al.astype(BF16), wup_ref[...]) + balpha_ref[...]
    gate_b = cat([f() for f in proj(wb_ref, OFF_GB, D_MODEL)])
    la = (jnp.minimum(z, 0.0) - jnp.log(1.0 + jnp.exp(-jnp.abs(z)))) * (1.0 / GATE_TAU)
    silu_b = gate_b * _sigmoid(gate_b)
    ys = [merge_out(rs, z_a) for rs, z_a in zip(halves, z_as)]
    gate_a = cat([f() for f in proj(wa_ref, OFF_GA, D_MODEL)])
    gsilu_s[...] = gate_a * _sigmoid(gate_a)

    n_fine = N_LEVELS - 1 - VEC_LEVELS
    odd_row = (lax.broadcasted_iota(jnp.int32, (CHUNK, DK), 0) & 1) == 1
    decs, e_gs, e_rs = [], [], []
    for c in range(n_chunks):
        la_c = la[c * CHUNK:(c + 1) * CHUNK, :]
        p0 = la_c.astype(BF16)
        rem = la_c - p0.astype(F32)
        p1 = rem.astype(BF16)
        p2 = (rem - p1.astype(F32)).astype(BF16)
        pieces = jnp.concatenate([p0, p1, p2], axis=0)
        expo = _dot(tmat_ref[...], pieces)
        g = expo[n_fine * CHUNK:, :]
        fine = jnp.exp(expo)
        pairs = jnp.where(odd_row, jnp.exp(la_c), 1.0)
        decs.append([jnp.exp(_block_exponents(g, l)) for l in range(VEC_LEVELS)]
                    + [fine[i * CHUNK:(i + 1) * CHUNK, :] for i in range(n_fine)] + [pairs])
        e_gs.append(fine[n_fine * CHUNK:, :])
        e_rs.append(jnp.exp(g[CHUNK - 1:CHUNK, :] - g))
    merge_blocks = proj(wb_ref, OFF_M, 2 * D_MODEL)
    merge_a = cat([f() for f in merge_blocks[:4]])
    for rs, y in zip(halves, ys):
        post_norm(rs, y)
    xprev_s[...] = x
    ga_s[...] = _sigmoid(merge_a + bmerge_ref[:, :D_MODEL])

    pbuf_s[POOL_HIST:POOL_HIST + tile, :] = u
    for g, w in enumerate(POOL_WINDOWS):
        cols = slice(g * POOL_GROUP_DIM, (g + 1) * POOL_GROUP_DIM)
        s = pbuf_s[:, cols]
        shift = 1
        while shift < w:
            s = s + pltpu.roll(s, shift, 0)
            shift *= 2
        s = s[POOL_HIST:POOL_HIST + tile, :]
        inv_head = jnp.where(j == 0, invhead_ref[:, cols], invw_ref[:, cols])
        pooled_s[0:POOL_HIST, cols] = (s[0:POOL_HIST, :] * inv_head
                                       - u[0:POOL_HIST, cols]).astype(BF16)
        pooled_s[POOL_HIST:, cols] = (s[POOL_HIST:, :] * invw_ref[:, cols]
                                      - u[POOL_HIST:, cols]).astype(BF16)
    pbuf_s[0:POOL_HIST, :] = pbuf_s[tile:tile + POOL_HIST, :]

    qgs, krs, a_lasts, scores = [], [], [], []
    for c in range(n_chunks):
        rows = slice(c * CHUNK, (c + 1) * CHUNK)
        dec = decs[c]
        q_c = q_s[rows, :]
        k_c = k_s[rows, :]
        qgs.append((q_c * e_gs[c]).astype(BF16))
        krs.append((k_c * e_rs[c]).astype(BF16))
        a_lasts.append(e_gs[c][CHUNK - 1:CHUNK, :])
        sc_c = []
        q_cb = q_c.astype(BF16)
        k_cb = k_c.astype(BF16)
        dec_b = [d.astype(BF16) for d in dec]
        for h in range(HEADS):
            lanes = slice(h * HEAD_DK, (h + 1) * HEAD_DK)
            acc = jnp.zeros((CHUNK, CHUNK), F32)
            for l in range(N_LEVELS + 1):
                if l < N_LEVELS:
                    a_l = q_cb[:, lanes] * dec_b[l][:, lanes]
                    b_l = k_cb[:, lanes] * dec_b[l][:, lanes]
                else:
                    a_l = q_cb[:, lanes]
                    b_l = k_cb[:, lanes]
                acc = acc + _dot(a_l, b_l.T) * masks_ref[l]
            sc_c.append(acc.astype(BF16))
        scores.append(sc_c)
    merge_b = [f() for f in merge_blocks[4:6]]
    mixed = cat([_dot(pooled_s[:, g * POOL_GROUP_DIM:(g + 1) * POOL_GROUP_DIM], wgrp_ref[g])
                 for g in range(len(POOL_WINDOWS))])
    merge_b += [f() for f in merge_blocks[6:]]
    y_b = (mixed * pscale_ref[...] * silu_b).astype(BF16)
    gate_mb = _sigmoid(cat(merge_b) + bmerge_ref[:, D_MODEL:])

    zb_blocks = [lambda c0=c0: _dot(y_b, wpb_ref[:, c0:c0 + PROJ_BLOCK])
                 for c0 in range(0, D_MODEL, PROJ_BLOCK)]
    zb_parts = []
    for c in range(n_chunks):
        rows = slice(c * CHUNK, (c + 1) * CHUNK)
        sts = [state_s[h] for h in range(HEADS)]
        for h in range(HEADS):
            kl = slice(h * HEAD_DK, (h + 1) * HEAD_DK)
            vl = slice(h * HEAD_DV, (h + 1) * HEAD_DV)
            lhs = jnp.concatenate([qgs[c][:, kl], scores[c][h]], axis=1)
            rhs = jnp.concatenate([sts[h].astype(BF16), v_s[rows, vl]], axis=0)
            o_s[rows, vl] = _dot(lhs, rhs)
        for h in range(HEADS):
            kl = slice(h * HEAD_DK, (h + 1) * HEAD_DK)
            vl = slice(h * HEAD_DV, (h + 1) * HEAD_DV)
            a_col = jnp.broadcast_to(a_lasts[c][:, kl], (HEAD_DK, HEAD_DK)).T
            a_col = jnp.concatenate([a_col, a_col], axis=1)
            state_s[h] = sts[h] * a_col + _dot_tn(krs[c][:, kl], v_s[rows, vl])
        n_fill = min(len(zb_blocks), -(-len(zb_blocks) * (c + 1) // (n_chunks - 1))) - len(zb_parts)
        zb_parts += [zb_blocks[len(zb_parts) + i]() for i in range(n_fill)]
    z_b = cat(zb_parts)

    gbzb_s[...] = gate_mb * z_b


def _const_spec(shape):
    zeros = (0,) * len(shape)
    return pl.BlockSpec(shape, lambda s: zeros, pipeline_mode=pl.Buffered(1))


def _layer_spec(shape, layer):
    zeros = (0,) * (len(shape) - 1)
    return pl.BlockSpec((None,) + tuple(shape[1:]), lambda s: (layer,) + zeros,
                        pipeline_mode=pl.Buffered(1))


def _layer_call(x, layer, params, consts, tile):
    batch, seq, _ = x.shape
    tiles_per_seq = seq // tile
    n_tiles = batch * tiles_per_seq

    def tile_index(t):
        return (t // tiles_per_seq, t % tiles_per_seq, 0)

    in_specs = [pl.BlockSpec((1, tile, D_MODEL), lambda s: tile_index(jnp.minimum(s, n_tiles - 1)))]
    in_specs += [_layer_spec(a.shape, layer) for a in params]
    in_specs += [_const_spec(a.shape) for a in consts]
    return pl.pallas_call(
        functools.partial(_layer_kernel, tiles_per_seq=tiles_per_seq),
        out_shape=jax.ShapeDtypeStruct(x.shape, x.dtype),
        grid=(n_tiles + 1,),
        in_specs=in_specs,
        out_specs=pl.BlockSpec((1, tile, D_MODEL), lambda s: tile_index(jnp.maximum(s - 1, 0))),
        scratch_shapes=[
            pltpu.VMEM((tile, DK), F32),
            pltpu.VMEM((tile, DK), F32),
            pltpu.VMEM((tile, DV), BF16),
            pltpu.VMEM((tile, DV), F32),
            pltpu.VMEM((HEADS, HEAD_DK, HEAD_DV), F32),
            pltpu.VMEM((tile + POOL_HIST, D_MODEL), F32),
            pltpu.VMEM((tile, D_MODEL), BF16),
            pltpu.VMEM((tile, D_MODEL), F32),
            pltpu.VMEM((tile, D_MODEL), F32),
            pltpu.VMEM((tile, D_MODEL), F32),
            pltpu.VMEM((tile, D_MODEL), F32),
            pltpu.VMEM((D_MODEL, _padded_cols(B_COLS)), BF16),
        ],
        compiler_params=pltpu.CompilerParams(
            dimension_semantics=("arbitrary",),
            vmem_limit_bytes=VMEM_LIMIT_BYTES),
        name="gla_pool_layer",
    )(x, *params, *consts)


def kernel(x, w_in, w_alpha_up, b_alpha, gla_norm_g, w_pool_grp, pool_scale, b_merge,
           w_proj_a, w_proj_b, w_out, ln_g, ln_b):
    batch, seq, d = x.shape
    depth = w_in.shape[0]
    assert d == D_MODEL and seq % SEQ_TILE == 0 and depth == DEPTH
    tmat, masks = _decay_tables()
    inv_w, inv_head = _pool_tables()
    consts = (jnp.asarray(tmat, BF16), jnp.asarray(masks, F32),
              jnp.asarray(inv_w, F32), jnp.asarray(inv_head, F32))
    def packed(w):
        pad = _padded_cols(w.shape[-1]) - w.shape[-1]
        return jnp.pad(w.astype(BF16), ((0, 0), (0, 0), (0, pad)))

    assert _padded_cols(w_in.shape[-1]) == -(-w_in.shape[-1] // LANE) * LANE
    w_up = jnp.pad(w_alpha_up, ((0, 0), (0, RANK_PAD - GATE_RANK), (0, 0)))
    params = (
        w_in.astype(BF16), w_up.astype(BF16),
        b_alpha.reshape(depth, 1, DK), gla_norm_g.reshape(depth, 1, DV),
        w_pool_grp.astype(BF16), pool_scale.reshape(depth, 1, D_MODEL),
        b_merge.reshape(depth, 1, 2 * D_MODEL),
        packed(w_proj_a), packed(w_proj_b), packed(w_out),
        ln_g.reshape(depth, 1, D_MODEL), ln_b.reshape(depth, 1, D_MODEL))
    for layer in range(depth):
        x = _layer_call(x, layer, params, consts, SEQ_TILE)
    return x
```

```python
import functools

import numpy as np
import jax
import jax.numpy as jnp
from jax import lax
from jax.experimental import pallas as pl
from jax.experimental.pallas import tpu as pltpu

D_MODEL = 1024
DEPTH = 4
CHUNK = 64
HEADS = 4
DK = D_MODEL // 2
DV = D_MODEL
HEAD_DK = DK // HEADS
HEAD_DV = DV // HEADS
GATE_RANK = 16
GATE_TAU = 16.0
POOL_WINDOWS = (2, 4, 8, 16)
POOL_GROUP_DIM = D_MODEL // len(POOL_WINDOWS)
POOL_HIST = 16
DEEPNORM_ALPHA = (2.0 * DEPTH) ** 0.25
EPS = 1e-5

LANE = 128
MXU_COLS = 256
RANK_PAD = LANE
PROJ_BLOCK = MXU_COLS
N_LEVELS = 6
SUBLANES = 8
VEC_LEVELS = 3
N_SPLIT = 3
SEQ_TILE = 256
EPILOGUE_PARTS = 1
REALIGN_ROWS = 128
VMEM_LIMIT_BYTES = 60 * 1024 * 1024

OFF_Q = 0
OFF_K = OFF_Q + DK
OFF_V = OFF_K + DK
OFF_GA = OFF_V + DV
OFF_AL = OFF_GA + DV
A_COLS = OFF_AL + GATE_RANK
OFF_PI = 0
OFF_GB = OFF_PI + D_MODEL
OFF_M = OFF_GB + D_MODEL
B_COLS = OFF_M + 2 * D_MODEL


def _padded_cols(n):
    tiles = -(-n // LANE)
    return (tiles + 1 - tiles % 2) * LANE

F32 = jnp.float32
BF16 = jnp.bfloat16


def _decay_tables():
    c = CHUNK
    t = np.zeros((N_LEVELS + 1, c, c), np.float32)
    masks = np.zeros((N_LEVELS + 1, c, c), np.float32)
    for l in range(N_LEVELS):
        n = c >> l
        half = n // 2
        for r in range(c):
            base = (r // n) * n
            b = base + half - 1
            if r > b:
                t[l, r, b + 1:r + 1] = 1.0
            else:
                t[l, r, r + 1:b + 1] = 1.0
            for s in range(base, base + n):
                if (r > b) != (s > b):
                    masks[l, r, s] = 1.0
    for r in range(c):
        t[N_LEVELS, r, :r + 1] = 1.0
    masks[N_LEVELS] = np.eye(c, dtype=np.float32)
    tmat = np.concatenate([t[VEC_LEVELS:N_LEVELS - 1], t[N_LEVELS:]]).reshape(-1, c)
    tmat = np.concatenate([tmat] * N_SPLIT, axis=1)
    masks = np.concatenate([masks, (masks[N_LEVELS - 1] + masks[N_LEVELS])[None]])
    return tmat, masks


def _pool_tables():
    inv_w = np.zeros((1, D_MODEL), np.float32)
    inv_head = np.zeros((POOL_HIST, D_MODEL), np.float32)
    for g, w in enumerate(POOL_WINDOWS):
        cols = slice(g * POOL_GROUP_DIM, (g + 1) * POOL_GROUP_DIM)
        inv_w[0, cols] = 1.0 / w
        for r in range(POOL_HIST):
            inv_head[r, cols] = 1.0 / min(r + 1, w)
    return inv_w, inv_head


def _dot(a, b):
    return jnp.dot(a, b, preferred_element_type=F32)


def _dot_tn(a, b):
    return lax.dot_general(a, b, (((0,), (0,)), ((), ())), preferred_element_type=F32)


def _block_exponents(g, level):
    n = CHUNK >> level
    half = n // 2
    assert half % SUBLANES == 0
    parts = []
    for base in range(0, CHUNK, n):
        g_b = g[base + half - 1:base + half, :]
        parts.append(g_b - g[base:base + half, :])
        parts.append(g[base + half:base + n, :] - g_b)
    return jnp.concatenate(parts, axis=0)


def _sigmoid(x):
    return 1.0 / (1.0 + jnp.exp(-x))


def _layer_kernel(x_ref, wa_ref, wup_ref, balpha_ref, gnorm_ref, wgrp_ref,
                  pscale_ref, bmerge_ref, wpa_ref, wpb_ref, wout_ref, lng_ref, lnb_ref,
                  tmat_ref, masks_ref, invw_ref, invhead_ref,
                  out_ref,
                  q_s, k_s, v_s, o_s, state_s, pbuf_s, pooled_s,
                  gsilu_s, ga_s, gbzb_s, xprev_s, wb_ref, *, tiles_per_seq):
    tile = x_ref.shape[1]
    n_chunks = tile // CHUNK
    step = pl.program_id(0)
    j = lax.rem(step, tiles_per_seq)

    @pl.when(step == 0)
    def _():
        for ref in (o_s, gsilu_s, ga_s, gbzb_s, xprev_s):
            ref[...] = jnp.zeros_like(ref)
        for r in range(0, D_MODEL, REALIGN_ROWS):
            wb_ref[r:r + REALIGN_ROWS, :B_COLS] = wa_ref[r:r + REALIGN_ROWS, A_COLS:A_COLS + B_COLS]

    @pl.when(j == 0)
    def _():
        state_s[...] = jnp.zeros_like(state_s)
        pbuf_s[0:POOL_HIST, :] = jnp.zeros((POOL_HIST, D_MODEL), F32)

    x = x_ref[0]
    xb = x.astype(BF16)
    halves = [slice(i * (tile // EPILOGUE_PARTS), (i + 1) * (tile // EPILOGUE_PARTS))
              for i in range(EPILOGUE_PARTS)]

    def branch_a(rs):
        ya_parts = []
        for h in range(HEADS):
            vl = slice(h * HEAD_DV, (h + 1) * HEAD_DV)
            o_h = o_s[rs, vl]
            ms = jnp.mean(o_h * o_h, axis=-1, keepdims=True)
            ya_parts.append(o_h * lax.rsqrt(ms + EPS) * gnorm_ref[:, vl])
        y_a = (jnp.concatenate(ya_parts, axis=1) * gsilu_s[rs, :]).astype(BF16)
        return _dot(y_a, wpa_ref[:, :D_MODEL])

    def merge_out(rs, z_a):
        merged = (ga_s[rs, :] * z_a + gbzb_s[rs, :]).astype(BF16)
        return _dot(merged, wout_ref[:, :D_MODEL])

    def post_norm(rs, y):
        r = DEEPNORM_ALPHA * xprev_s[rs, :] + y
        mu = jnp.mean(r, axis=-1, keepdims=True)
        d = r - mu
        var = jnp.mean(d * d, axis=-1, keepdims=True)
        out_ref[0, rs, :] = d * lax.rsqrt(var + EPS) * lng_ref[...] + lnb_ref[...]

    def proj(w_ref, col0, ncols):
        return [lambda c0=c0: _dot(xb, w_ref[:, c0:c0 + PROJ_BLOCK])
                for c0 in range(col0, col0 + ncols, PROJ_BLOCK)]

    def cat(parts):
        return jnp.concatenate(parts, axis=1)

    qk = _dot(xb, wa_ref[:, OFF_Q:OFF_V])
    q_s[...] = qk[:, :DK] * (HEAD_DK ** -0.5)
    k_s[...] = qk[:, DK:]
    v_s[...] = _dot(xb, wa_ref[:, OFF_V:OFF_GA]).astype(BF16)
    al = _dot(xb, wa_ref[:, OFF_AL:OFF_AL + RANK_PAD])
    z_as = [branch_a(rs) for rs in halves]
    u = cat([f() for f in proj(wb_ref, OFF_PI, D_MODEL)])
    z = _dot(al.astype(BF16), wup_ref[...]) + balpha_ref[...]
    gate_b = cat([f() for f in proj(wb_ref, OFF_GB, D_MODEL)])
    la = (jnp.minimum(z, 0.0) - jnp.log(1.0 + jnp.exp(-jnp.abs(z)))) * (1.0 / GATE_TAU)
    silu_b = gate_b * _sigmoid(gate_b)
    ys = [merge_out(rs, z_a) for rs, z_a in zip(halves, z_as)]
    gate_a = cat([f() for f in proj(wa_ref, OFF_GA, D_MODEL)])
    gsilu_s[...] = gate_a * _sigmoid(gate_a)

    n_fine = N_LEVELS - 1 - VEC_LEVELS
    odd_row = (lax.broadcasted_iota(jnp.int32, (CHUNK, DK), 0) & 1) == 1
    decs, e_gs, e_rs = [], [], []
    for c in range(n_chunks):
        la_c = la[c * CHUNK:(c + 1) * CHUNK, :]
        p0 = la_c.astype(BF16)
        rem = la_c - p0.astype(F32)
        p1 = rem.astype(BF16)
        p2 = (rem - p1.astype(F32)).astype(BF16)
        pieces = jnp.concatenate([p0, p1, p2], axis=0)
        expo = _dot(tmat_ref[...], pieces)
        g = expo[n_fine * CHUNK:, :]
        fine = jnp.exp(expo)
        pairs = jnp.where(odd_row, jnp.exp(la_c), 1.0)
        decs.append([jnp.exp(_block_exponents(g, l)) for l in range(VEC_LEVELS)]
                    + [fine[i * CHUNK:(i + 1) * CHUNK, :] for i in range(n_fine)] + [pairs])
        e_gs.append(fine[n_fine * CHUNK:, :])
        e_rs.append(jnp.exp(g[CHUNK - 1:CHUNK, :] - g))
    merge_blocks = proj(wb_ref, OFF_M, 2 * D_MODEL)
    merge_a = cat([f() for f in merge_blocks[:4]])
    for rs, y in zip(halves, ys):
        post_norm(rs, y)
    xprev_s[...] = x
    ga_s[...] = _sigmoid(merge_a + bmerge_ref[:, :D_MODEL])

    pbuf_s[POOL_HIST:POOL_HIST + tile, :] = u
    for g, w in enumerate(POOL_WINDOWS):
        cols = slice(g * POOL_GROUP_DIM, (g + 1) * POOL_GROUP_DIM)
        s = pbuf_s[:, cols]
        shift = 1
        while shift < w:
            s = s + pltpu.roll(s, shift, 0)
            shift *= 2
        s = s[POOL_HIST:POOL_HIST + tile, :]
        inv_head = jnp.where(j == 0, invhead_ref[:, cols], invw_ref[:, cols])
        pooled_s[0:POOL_HIST, cols] = (s[0:POOL_HIST, :] * inv_head
                                       - u[0:POOL_HIST, cols]).astype(BF16)
        pooled_s[POOL_HIST:, cols] = (s[POOL_HIST:, :] * invw_ref[:, cols]
                                      - u[POOL_HIST:, cols]).astype(BF16)
    pbuf_s[0:POOL_HIST, :] = pbuf_s[tile:tile + POOL_HIST, :]

    qgs, krs, a_lasts, scores = [], [], [], []
    for c in range(n_chunks):
        rows = slice(c * CHUNK, (c + 1) * CHUNK)
        dec = decs[c]
        q_c = q_s[rows, :]
        k_c = k_s[rows, :]
        qgs.append((q_c * e_gs[c]).astype(BF16))
        krs.append((k_c * e_rs[c]).astype(BF16))
        a_lasts.append(e_gs[c][CHUNK - 1:CHUNK, :])
        sc_c = []
        q_cb = q_c.astype(BF16)
        k_cb = k_c.astype(BF16)
        dec_b = [d.astype(BF16) for d in dec]
        for h in range(HEADS):
            lanes = slice(h * HEAD_DK, (h + 1) * HEAD_DK)
            acc = jnp.zeros((CHUNK, CHUNK), F32)
            for l in range(N_LEVELS):
                if l < N_LEVELS - 1:
                    a_l = q_cb[:, lanes] * dec_b[l][:, lanes]
                    b_l = k_cb[:, lanes] * dec_b[l][:, lanes]
                    m_l = masks_ref[l]
                else:
                    odd_h = (lax.broadcasted_iota(jnp.int32, (CHUNK, HEAD_DK), 0) & 1) == 1
                    x_odd = dec[l][:, lanes]
                    x_next = jnp.concatenate([x_odd[1:, :], x_odd[:1, :]], axis=0)
                    q_h = q_c[:, lanes]
                    k_h = k_c[:, lanes]
                    a_l = jnp.concatenate(
                        [(q_h * x_odd).astype(BF16),
                         (q_h * jnp.where(odd_h, 1.0, x_next)).astype(BF16)], axis=1)
                    b_l = jnp.concatenate([jnp.where(odd_h, 0.0, k_h).astype(BF16),
                                           jnp.where(odd_h, k_h, 0.0).astype(BF16)], axis=1)
                    m_l = masks_ref[N_LEVELS + 1]
                acc = acc + _dot(a_l, b_l.T) * m_l
            sc_c.append(acc.astype(BF16))
        scores.append(sc_c)
    merge_b = [f() for f in merge_blocks[4:6]]
    mixed = cat([_dot(pooled_s[:, g * POOL_GROUP_DIM:(g + 1) * POOL_GROUP_DIM], wgrp_ref[g])
                 for g in range(len(POOL_WINDOWS))])
    merge_b += [f() for f in merge_blocks[6:]]
    y_b = (mixed * pscale_ref[...] * silu_b).astype(BF16)
    gate_mb = _sigmoid(cat(merge_b) + bmerge_ref[:, D_MODEL:])

    zb_blocks = [lambda c0=c0: _dot(y_b, wpb_ref[:, c0:c0 + PROJ_BLOCK])
                 for c0 in range(0, D_MODEL, PROJ_BLOCK)]
    zb_parts = []
    for c in range(n_chunks):
        rows = slice(c * CHUNK, (c + 1) * CHUNK)
        sts = [state_s[h] for h in range(HEADS)]
        for h in range(HEADS):
            kl = slice(h * HEAD_DK, (h + 1) * HEAD_DK)
            vl = slice(h * HEAD_DV, (h + 1) * HEAD_DV)
            lhs = jnp.concatenate([qgs[c][:, kl], scores[c][h]], axis=1)
            rhs = jnp.concatenate([sts[h].astype(BF16), v_s[rows, vl]], axis=0)
            o_s[rows, vl] = _dot(lhs, rhs)
        for h in range(HEADS):
            kl = slice(h * HEAD_DK, (h + 1) * HEAD_DK)
            vl = slice(h * HEAD_DV, (h + 1) * HEAD_DV)
            a_col = jnp.broadcast_to(a_lasts[c][:, kl], (HEAD_DK, HEAD_DK)).T
            a_col = jnp.concatenate([a_col, a_col], axis=1)
            state_s[h] = sts[h] * a_col + _dot_tn(krs[c][:, kl], v_s[rows, vl])
        n_fill = min(len(zb_blocks), -(-len(zb_blocks) * (c + 1) // (n_chunks - 1))) - len(zb_parts)
        zb_parts += [zb_blocks[len(zb_parts) + i]() for i in range(n_fill)]
    z_b = cat(zb_parts)

    gbzb_s[...] = gate_mb * z_b


def _const_spec(shape):
    zeros = (0,) * len(shape)
    return pl.BlockSpec(shape, lambda s: zeros, pipeline_mode=pl.Buffered(1))


def _layer_spec(shape, layer):
    zeros = (0,) * (len(shape) - 1)
    return pl.BlockSpec((None,) + tuple(shape[1:]), lambda s: (layer,) + zeros,
                        pipeline_mode=pl.Buffered(1))


def _layer_call(x, layer, params, consts, tile):
    batch, seq, _ = x.shape
    tiles_per_seq = seq // tile
    n_tiles = batch * tiles_per_seq

    def tile_index(t):
        return (t // tiles_per_seq, t % tiles_per_seq, 0)

    in_specs = [pl.BlockSpec((1, tile, D_MODEL), lambda s: tile_index(jnp.minimum(s, n_tiles - 1)))]
    in_specs += [_layer_spec(a.shape, layer) for a in params]
    in_specs += [_const_spec(a.shape) for a in consts]
    return pl.pallas_call(
        functools.partial(_layer_kernel, tiles_per_seq=tiles_per_seq),
        out_shape=jax.ShapeDtypeStruct(x.shape, x.dtype),
        grid=(n_tiles + 1,),
        in_specs=in_specs,
        out_specs=pl.BlockSpec((1, tile, D_MODEL), lambda s: tile_index(jnp.maximum(s - 1, 0))),
        scratch_shapes=[
            pltpu.VMEM((tile, DK), F32),
            pltpu.VMEM((tile, DK), F32),
            pltpu.VMEM((tile, DV), BF16),
            pltpu.VMEM((tile, DV), F32),
            pltpu.VMEM((HEADS, HEAD_DK, HEAD_DV), F32),
            pltpu.VMEM((tile + POOL_HIST, D_MODEL), F32),
            pltpu.VMEM((tile, D_MODEL), BF16),
            pltpu.VMEM((tile, D_MODEL), F32),
            pltpu.VMEM((tile, D_MODEL), F32),
            pltpu.VMEM((tile, D_MODEL), F32),
            pltpu.VMEM((tile, D_MODEL), F32),
            pltpu.VMEM((D_MODEL, _padded_cols(B_COLS)), BF16),
        ],
        compiler_params=pltpu.CompilerParams(
            dimension_semantics=("arbitrary",),
            vmem_limit_bytes=VMEM_LIMIT_BYTES),
        name="gla_pool_layer",
    )(x, *params, *consts)


def kernel(x, w_in, w_alpha_up, b_alpha, gla_norm_g, w_pool_grp, pool_scale, b_merge,
           w_proj_a, w_proj_b, w_out, ln_g, ln_b):
    batch, seq, d = x.shape
    depth = w_in.shape[0]
    assert d == D_MODEL and seq % SEQ_TILE == 0 and depth == DEPTH
    tmat, masks = _decay_tables()
    inv_w, inv_head = _pool_tables()
    consts = (jnp.asarray(tmat, BF16), jnp.asarray(masks, F32),
              jnp.asarray(inv_w, F32), jnp.asarray(inv_head, F32))
    def packed(w):
        pad = _padded_cols(w.shape[-1]) - w.shape[-1]
        return jnp.pad(w.astype(BF16), ((0, 0), (0, 0), (0, pad)))

    assert _padded_cols(w_in.shape[-1]) == -(-w_in.shape[-1] // LANE) * LANE
    w_up = jnp.pad(w_alpha_up, ((0, 0), (0, RANK_PAD - GATE_RANK), (0, 0)))
    params = (
        w_in.astype(BF16), w_up.astype(BF16),
        b_alpha.reshape(depth, 1, DK), gla_norm_g.reshape(depth, 1, DV),
        w_pool_grp.astype(BF16), pool_scale.reshape(depth, 1, D_MODEL),
        b_merge.reshape(depth, 1, 2 * D_MODEL),
        packed(w_proj_a), packed(w_proj_b), packed(w_out),
        ln_g.reshape(depth, 1, D_MODEL), ln_b.reshape(depth, 1, D_MODEL))
    for layer in range(depth):
        x = _layer_call(x, layer, params, consts, SEQ_TILE)
    return x
```
